```python
import math
import jax
import jax.numpy as jnp
from jax import lax
import numpy as np

D_MODEL = 1024
BATCH = 2
SEQ = 8192
DEPTH = 1
DEC_BATCH = 128
DEC_SEQ = 8
PAST_LEN = 2048
PAGE_SIZE = 128

N_META = 16
H_A = 4
DH_A = 64
H_B = 4
DH_B = 128
H_I = 8
D_I = 64
TOPK_MAX = 256
ROPE_THETA = 500000.0
ROT_A = DH_A // 4
ROT_B = DH_B // 4
ROT_I = D_I // 4
N_GROUPS = 4
EXP_PER_GROUP = 8
N_EXPERTS = N_GROUPS * EXP_PER_GROUP
TOP_E = 2
D_FF_E = 256
Q_BLOCK = 128
DN_ALPHA = (2.0 * DEPTH) ** 0.25
DN_BETA = (8.0 * DEPTH) ** -0.25
LN_EPS = 1e-5
RMS_EPS = 1e-6
W_A_DIM = H_A * 2 * DH_A
W_B_DIM = H_B * DH_B
IN_WIDTHS = (W_A_DIM, W_A_DIM, W_A_DIM, W_B_DIM, W_B_DIM, W_B_DIM, H_I * D_I, D_I, H_I, D_MODEL, D_MODEL)

kernel_name = 'hybrid_diffattn_dsa_hmoe_step'


def layer_norm(x, g, b):
    xf = x.astype(jnp.float32)
    mu = jnp.mean(xf, axis=-1, keepdims=True)
    var = jnp.mean(jnp.square(xf - mu), axis=-1, keepdims=True)
    y = (xf - mu) * lax.rsqrt(var + LN_EPS) * g.astype(jnp.float32) + b.astype(jnp.float32)
    return y.astype(x.dtype)


def partial_rope(x, pos, rot):
    half = rot // 2
    inv = jnp.power(ROPE_THETA, -jnp.arange(half, dtype=jnp.float32) / half)
    ang = pos.astype(jnp.float32)[:, None] * inv[None, :]
    cos = jnp.cos(ang)[:, None, :]
    sin = jnp.sin(ang)[:, None, :]
    xf = x.astype(jnp.float32)
    x1, x2, rest = xf[..., :half], xf[..., half:rot], xf[..., rot:]
    return jnp.concatenate([x1 * cos - x2 * sin, x2 * cos + x1 * sin, rest], axis=-1).astype(x.dtype)


def in_project(h, pos, w_in):
    B, T, _ = h.shape
    p = jnp.einsum('btd,dc->btc', h, w_in)
    cuts = np.cumsum(IN_WIDTHS)[:-1].tolist()
    qa, ka, va, qb, kb, vb, qi, ki, wi, ga, gb = jnp.split(p, cuts, axis=-1)
    qa = partial_rope(qa.reshape(B, T, 2 * H_A, DH_A), pos, ROT_A).reshape(B, T, H_A, 2, DH_A)
    ka = partial_rope(ka.reshape(B, T, 2 * H_A, DH_A), pos, ROT_A).reshape(B, T, H_A, 2 * DH_A)
    va = va.reshape(B, T, H_A, 2 * DH_A)
    qb = partial_rope(qb.reshape(B, T, H_B, DH_B), pos, ROT_B)
    kb = partial_rope(kb.reshape(B, T, H_B, DH_B), pos, ROT_B)
    vb = vb.reshape(B, T, H_B, DH_B)
    qi = partial_rope(qi.reshape(B, T, H_I, D_I), pos, ROT_I)
    ki = partial_rope(ki.reshape(B, T, 1, D_I), pos, ROT_I)[:, :, 0]
    wi = wi * (H_I ** -0.5)
    return qa, ka, va, qb, kb, vb, qi, ki, wi, ga, gb


def joint_softmax(scores, masks):
    s = jnp.concatenate([jnp.where(m, sc, -jnp.inf) for sc, m in zip(scores, masks)], axis=-1)
    p = jax.nn.softmax(s, axis=-1)
    cuts = np.cumsum([sc.shape[-1] for sc in scores])[:-1].tolist()
    return jnp.split(p, cuts, axis=-1)


def diff_lambda(lq1, lk1, lq2, lk2, lam_init):
    f = jnp.float32
    return jnp.exp(jnp.sum(lq1.astype(f) * lk1.astype(f))) - jnp.exp(jnp.sum(lq2.astype(f) * lk2.astype(f))) + lam_init


def diff_attend(q, segs, lam):
    scores = [jnp.einsum('bqhmd,bshmd->bhmqs', q, k.reshape(k.shape[:3] + (2, DH_A))).astype(jnp.float32) * (DH_A ** -0.5)
              for k, _, _ in segs]
    probs = joint_softmax(scores, [m for _, _, m in segs])
    out = 0
    for p, (_, v, _) in zip(probs, segs):
        w = (p[:, :, 0] - lam * p[:, :, 1]).astype(v.dtype)
        out = out + jnp.einsum('bhqs,bshe->bqhe', w, v)
    return out


def diff_out(o, gain, lam_init):
    of = o.astype(jnp.float32)
    of = of * lax.rsqrt(jnp.mean(of * of, axis=-1, keepdims=True) + RMS_EPS) * gain.astype(jnp.float32) * (1.0 - lam_init)
    return of.astype(o.dtype).reshape(o.shape[:2] + (W_A_DIM,))


def to_blocks(a, nblk):
    pad = nblk * Q_BLOCK - a.shape[1]
    a = jnp.pad(a, [(0, 0), (0, pad)] + [(0, 0)] * (a.ndim - 2))
    a = a.reshape((a.shape[0], nblk, Q_BLOCK) + a.shape[2:])
    return jnp.moveaxis(a, 1, 0)


def from_blocks(o, T):
    o = jnp.moveaxis(o, 0, 1)
    return o.reshape((o.shape[0], -1) + o.shape[3:])[:, :T]


def diff_prompt(q, k, v, lam):
    T = q.shape[1]
    nblk = -(-T // Q_BLOCK)
    kpos = jnp.arange(T)

    def one(args):
        qb, i = args
        qpos = i * Q_BLOCK + jnp.arange(Q_BLOCK)
        mask = (kpos[None, :] <= qpos[:, None])[None, None, None]
        return diff_attend(qb, [(k, v, mask)], lam)

    o = lax.map(one, (to_blocks(q, nblk), jnp.arange(nblk)))
    return from_blocks(o, T)


def diff_sample(q, k_new, v_new, pool_k, pool_v, page_table, lam):
    Bd, T = q.shape[:2]
    past = page_table.shape[1] * PAGE_SIZE
    k_past = pool_k[page_table].reshape(Bd, past, H_A, 2 * DH_A)
    v_past = pool_v[page_table].reshape(Bd, past, H_A, 2 * DH_A)
    m_past = jnp.ones((1, 1, 1, T, past), dtype=bool)
    m_new = (jnp.arange(T)[None, :] <= jnp.arange(T)[:, None])[None, None, None]
    return diff_attend(q, [(k_past, v_past, m_past), (k_new, v_new, m_new)], lam)


def indexer_scores(qi, wi, ki):
    r = jax.nn.relu(jnp.einsum('bqhd,bsd->bqhs', qi, ki).astype(jnp.float32) * (D_I ** -0.5))
    return jnp.einsum('bqhs,bqh->bqs', r, wi.astype(jnp.float32))


def dsa_prompt(q, k, v, qi, wi, ki, topk):
    T = q.shape[1]
    nblk = -(-T // Q_BLOCK)
    kpos = jnp.arange(T)

    def one(args):
        qb, qib, wib, i = args
        qpos = i * Q_BLOCK + jnp.arange(Q_BLOCK)
        sc = indexer_scores(qib, wib, ki)
        sc = jnp.where((kpos[None, :] <= qpos[:, None])[None], sc, -jnp.inf)
        _, idx = lax.top_k(sc, topk)
        valid = idx <= qpos[None, :, None]
        kg = jax.vmap(lambda a, ix: a[ix])(k, idx)
        vg = jax.vmap(lambda a, ix: a[ix])(v, idx)
        s = jnp.einsum('bqhd,bqkhd->bhqk', qb, kg).astype(jnp.float32) * (DH_B ** -0.5)
        (p,) = joint_softmax([s], [valid[:, None]])
        return jnp.einsum('bhqk,bqkhd->bqhd', p.astype(vg.dtype), vg)

    o = lax.map(one, (to_blocks(q, nblk), to_blocks(qi, nblk), to_blocks(wi, nblk), jnp.arange(nblk)))
    return from_blocks(o, T)


def dsa_sample(q, k_new, v_new, qi, wi, ki_new, pool_k, pool_v, pool_ki, page_table, topk):
    Bd, T = q.shape[:2]
    past = page_table.shape[1] * PAGE_SIZE
    ki_all = jnp.concatenate([pool_ki[page_table].reshape(Bd, past, D_I), ki_new], axis=1)
    qpos = past + jnp.arange(T)
    kpos = jnp.arange(past + T)
    sc = indexer_scores(qi, wi, ki_all)
    sc = jnp.where((kpos[None, :] <= qpos[:, None])[None], sc, -jnp.inf)
    _, idx = lax.top_k(sc, topk)
    valid = idx <= qpos[None, :, None]
    ip = jnp.minimum(idx, past - 1)
    phys = jax.vmap(lambda pt, ii: pt[ii])(page_table, ip // PAGE_SIZE)
    row = phys * PAGE_SIZE + ip % PAGE_SIZE
    kg = pool_k.reshape(-1, H_B, DH_B)[row]
    vg = pool_v.reshape(-1, H_B, DH_B)[row]
    sel_past = valid & (idx < past)
    sel_new = jnp.any((idx[..., None] == (past + jnp.arange(T))) & valid[..., None], axis=2)
    s_past = jnp.einsum('bqhd,bqkhd->bhqk', q, kg).astype(jnp.float32) * (DH_B ** -0.5)
    s_new = jnp.einsum('bqhd,bjhd->bhqj', q, k_new).astype(jnp.float32) * (DH_B ** -0.5)
    p_past, p_new = joint_softmax([s_past, s_new], [sel_past[:, None], sel_new[:, None]])
    return (jnp.einsum('bhqk,bqkhd->bqhd', p_past.astype(vg.dtype), vg)
            + jnp.einsum('bhqj,bjhd->bqhd', p_new.astype(v_new.dtype), v_new))


def merge_branches(oa, ob, ga, gb, w_a, w_b, w_o):
    ya = jnp.einsum('btc,cd->btd', oa, w_a)
    yb = jnp.einsum('btc,cd->btd', ob, w_b)
    m = jax.nn.sigmoid(ga) * ya + jax.nn.sigmoid(gb) * yb
    return jnp.einsum('btd,de->bte', m, w_o)


def hier_moe(h, w_grp, b_grp, w_exp, b_exp, w_gate, w_up, w_down):
    shp = h.shape
    hf = h.reshape(-1, D_MODEL)
    pg = jax.nn.softmax((hf @ w_grp + b_grp).astype(jnp.float32), axis=-1)
    g_val, g_idx = lax.top_k(pg, 1)
    g_hot = jax.nn.one_hot(g_idx[:, 0], N_GROUPS, dtype=jnp.float32)
    le = (hf @ w_exp + b_exp).astype(jnp.float32).reshape(-1, N_GROUPS, EXP_PER_GROUP)
    pe = jax.nn.softmax(jnp.einsum('nge,ng->ne', le, g_hot), axis=-1)
    e_val, e_idx = lax.top_k(pe, TOP_E)
    gate = g_val * e_val / jnp.sum(e_val, axis=-1, keepdims=True)
    eid = g_idx * EXP_PER_GROUP + e_idx
    comb = jnp.einsum('nke,nk->ne', jax.nn.one_hot(eid, N_EXPERTS, dtype=jnp.float32), gate).astype(h.dtype)
    hg = jnp.einsum('nd,edf->nef', hf, w_gate)
    hu = jnp.einsum('nd,edf->nef', hf, w_up)
    y = jnp.einsum('nef,efd->nd', jax.nn.silu(hg) * hu * comb[:, :, None], w_down)
    return y.reshape(shp)


def setup_inputs(seed: int = 0) -> dict:
    key = jax.random.key(seed)
    ks = jax.random.split(key, 32)
    f32 = jnp.float32

    def nrm(k, shape, scale):
        return jax.random.normal(k, shape, f32) * scale

    n_pages = PAST_LEN // PAGE_SIZE
    n_used = DEC_BATCH * n_pages
    n_pool = n_used + (n_used + 3) // 4
    perm = jax.random.permutation(ks[0], n_pool)
    page_table = perm[:n_used].reshape(DEC_BATCH, n_pages).astype(jnp.int32)
    s_in = D_MODEL ** -0.5
    in_scales = (s_in, s_in, s_in * DN_BETA, s_in, s_in, s_in * DN_BETA, s_in, s_in, s_in, s_in, s_in)
    kin = jax.random.split(ks[8], len(IN_WIDTHS))
    w_in = jnp.concatenate([nrm(kk, (DEPTH, D_MODEL, wd), sc) for kk, wd, sc in zip(kin, IN_WIDTHS, in_scales)], axis=-1)
    return {
        'x_prompt': nrm(ks[1], (BATCH, SEQ, D_MODEL), 1.0),
        'x_sample': nrm(ks[2], (DEC_BATCH, DEC_SEQ, D_MODEL), 1.0),
        'cache_diff_k': nrm(ks[3], (DEPTH, n_pool, PAGE_SIZE, H_A, 2 * DH_A), 1.0),
        'cache_diff_v': nrm(ks[4], (DEPTH, n_pool, PAGE_SIZE, H_A, 2 * DH_A), 1.0),
        'cache_dsa_k': nrm(ks[5], (DEPTH, n_pool, PAGE_SIZE, H_B, DH_B), 1.0),
        'cache_dsa_v': nrm(ks[6], (DEPTH, n_pool, PAGE_SIZE, H_B, DH_B), 1.0),
        'cache_idx_k': nrm(ks[7], (DEPTH, n_pool, PAGE_SIZE, D_I), 1.0),
        'page_table': page_table,
        'meta_tokens': nrm(ks[9], (N_META, D_MODEL), 1.0),
        'w_in': w_in,
        'lam_q1': nrm(ks[10], (DEPTH, DH_A), 0.1),
        'lam_k1': nrm(ks[11], (DEPTH, DH_A), 0.1),
        'lam_q2': nrm(ks[12], (DEPTH, DH_A), 0.1),
        'lam_k2': nrm(ks[13], (DEPTH, DH_A), 0.1),
        'diff_norm_g': 1.0 + nrm(ks[14], (DEPTH, 2 * DH_A), 0.02),
        'w_branch_a': nrm(ks[15], (DEPTH, W_A_DIM, D_MODEL), W_A_DIM ** -0.5),
        'w_branch_b': nrm(ks[16], (DEPTH, W_B_DIM, D_MODEL), W_B_DIM ** -0.5),
        'w_out': nrm(ks[17], (DEPTH, D_MODEL, D_MODEL), s_in * DN_BETA),
        'ln1_g': 1.0 + nrm(ks[18], (DEPTH, D_MODEL), 0.02),
        'ln1_b': nrm(ks[19], (DEPTH, D_MODEL), 0.02),
        'w_grp': nrm(ks[20], (DEPTH, D_MODEL, N_GROUPS), s_in),
        'b_grp': nrm(ks[21], (DEPTH, N_GROUPS), 0.01),
        'w_exp': nrm(ks[22], (DEPTH, D_MODEL, N_EXPERTS), s_in),
        'b_exp': nrm(ks[23], (DEPTH, N_EXPERTS), 0.01),
        'w_gate': nrm(ks[24], (DEPTH, N_EXPERTS, D_MODEL, D_FF_E), s_in),
        'w_up': nrm(ks[25], (DEPTH, N_EXPERTS, D_MODEL, D_FF_E), s_in),
        'w_down': nrm(ks[26], (DEPTH, N_EXPERTS, D_FF_E, D_MODEL), (D_FF_E ** -0.5) * DN_BETA),
        'ln2_g': 1.0 + nrm(ks[27], (DEPTH, D_MODEL), 0.02),
        'ln2_b': nrm(ks[28], (DEPTH, D_MODEL), 0.02),
    }


def reference(x_prompt, x_sample, cache_diff_k, cache_diff_v, cache_dsa_k, cache_dsa_v, cache_idx_k, page_table,
              meta_tokens, w_in, lam_q1, lam_k1, lam_q2, lam_k2, diff_norm_g, w_branch_a, w_branch_b, w_out,
              ln1_g, ln1_b, w_grp, b_grp, w_exp, b_exp, w_gate, w_up, w_down, ln2_g, ln2_b):
    B = x_prompt.shape[0]
    meta = jnp.broadcast_to(meta_tokens[None].astype(x_prompt.dtype), (B, N_META, D_MODEL))
    xp = jnp.concatenate([meta, x_prompt], axis=1)
    T_p = xp.shape[1]
    pos_p = jnp.arange(T_p, dtype=jnp.int32)
    past = page_table.shape[1] * PAGE_SIZE
    xs = x_sample
    T_s = xs.shape[1]
    pos_s = past + jnp.arange(T_s, dtype=jnp.int32)
    topk_p = min(TOPK_MAX, x_prompt.shape[1] // 4)
    topk_s = min(TOPK_MAX, (past + T_s) // 4)
    rows_p, rows_s = [], []
    for l in range(DEPTH):
        lam_init = 0.8 - 0.6 * math.exp(-0.3 * l)
        lam = diff_lambda(lam_q1[l], lam_k1[l], lam_q2[l], lam_k2[l], lam_init)
        qa, ka, va, qb, kb, vb, qi, ki, wi, ga, gb = in_project(xp, pos_p, w_in[l])
        oa = diff_out(diff_prompt(qa, ka, va, lam), diff_norm_g[l], lam_init)
        ob = dsa_prompt(qb, kb, vb, qi, wi, ki, topk_p).reshape(B, T_p, W_B_DIM)
        mix_p = merge_branches(oa, ob, ga, gb, w_branch_a[l], w_branch_b[l], w_out[l])
        rows_p.append((ka, va, kb, vb, ki))
        qa, ka, va, qb, kb, vb, qi, ki, wi, ga, gb = in_project(xs, pos_s, w_in[l])
        oa = diff_out(diff_sample(qa, ka, va, cache_diff_k[l], cache_diff_v[l], page_table, lam), diff_norm_g[l], lam_init)
        ob = dsa_sample(qb, kb, vb, qi, wi, ki, cache_dsa_k[l], cache_dsa_v[l], cache_idx_k[l], page_table,
                        topk_s).reshape(xs.shape[0], T_s, W_B_DIM)
        mix_s = merge_branches(oa, ob, ga, gb, w_branch_a[l], w_branch_b[l], w_out[l])
        rows_s.append((ka, va, kb, vb, ki))
        xp = layer_norm(DN_ALPHA * xp + mix_p, ln1_g[l], ln1_b[l])
        xs = layer_norm(DN_ALPHA * xs + mix_s, ln1_g[l], ln1_b[l])
        xp = layer_norm(DN_ALPHA * xp + hier_moe(xp, w_grp[l], b_grp[l], w_exp[l], b_exp[l], w_gate[l], w_up[l], w_down[l]),
                        ln2_g[l], ln2_b[l])
        xs = layer_norm(DN_ALPHA * xs + hier_moe(xs, w_grp[l], b_grp[l], w_exp[l], b_exp[l], w_gate[l], w_up[l], w_down[l]),
                        ln2_g[l], ln2_b[l])
    y_prompt = xp[:, N_META:]
    y_sample = xs
    new_diff_k_p, new_diff_v_p, new_dsa_k_p, new_dsa_v_p, new_idx_k_p = [jnp.stack(r) for r in zip(*rows_p)]
    new_diff_k_s, new_diff_v_s, new_dsa_k_s, new_dsa_v_s, new_idx_k_s = [jnp.stack(r) for r in zip(*rows_s)]
    return (y_prompt, y_sample, new_diff_k_p, new_diff_v_p, new_dsa_k_p, new_dsa_v_p, new_idx_k_p,
            new_diff_k_s, new_diff_v_s, new_dsa_k_s, new_dsa_v_s, new_idx_k_s)
```

```python
import functools
import math

import numpy as np
import jax
import jax.numpy as jnp
from jax import lax
from jax.experimental import pallas as pl
from jax.experimental.pallas import tpu as pltpu

F32 = jnp.float32
BF16 = jnp.bfloat16
I32 = jnp.int32

N_META = 16
H_A = 4
DH_A = 64
H_B = 4
DH_B = 128
H_I = 8
D_I = 64
TOPK_MAX = 256
ROPE_THETA = 500000.0
N_GROUPS = 4
EXP_PER_GROUP = 8
N_EXPERTS = N_GROUPS * EXP_PER_GROUP
LN_EPS = 1e-5
RMS_EPS = 1e-6
W_A_DIM = H_A * 2 * DH_A
W_B_DIM = H_B * DH_B
IN_WIDTHS = (W_A_DIM, W_A_DIM, W_A_DIM, W_B_DIM, W_B_DIM, W_B_DIM, H_I * D_I, D_I, H_I, None, None)

LANES = 128
KEY_CHUNK = 128
NEG = -1e30
INT_MIN = -(2 ** 31)
VMEM_LIMIT = 56 * 1024 * 1024

NT_DIMS = (((1,), (1,)), ((), ()))


def _cparams(sem):
    return pltpu.CompilerParams(dimension_semantics=sem, vmem_limit_bytes=VMEM_LIMIT)


def _full_spec(a):
    nd = a.ndim
    return pl.BlockSpec(a.shape, lambda *_: (0,) * nd)


def _rope_tables(pos, rot, period):
    half = rot // 2
    inv = jnp.power(ROPE_THETA, -jnp.arange(half, dtype=F32) / half)
    ang = pos.astype(F32)[:, None] * inv[None, :]
    cos, sin = jnp.cos(ang), jnp.sin(ang)
    n = pos.shape[0]
    zh = jnp.zeros((n, half), F32)
    rest0 = jnp.zeros((n, period - rot), F32)
    c = jnp.concatenate([cos, cos, jnp.ones((n, period - rot), F32)], axis=1)
    sn = jnp.concatenate([-sin, zh, rest0], axis=1)
    sp = jnp.concatenate([zh, sin, rest0], axis=1)
    reps = LANES // period
    return tuple(jnp.tile(t, (1, reps)) for t in (c, sn, sp))


def _rope(y, c, sn, sp, half):
    outs = []
    for j in range(y.shape[1] // LANES):
        ys = y[:, j * LANES:(j + 1) * LANES]
        outs.append(ys * c + pltpu.roll(ys, LANES - half, 1) * sn + pltpu.roll(ys, half, 1) * sp)
    return outs[0] if len(outs) == 1 else jnp.concatenate(outs, axis=1)


def _proj_body(*refs, segs):
    x_ref = refs[0]
    tabs = refs[1:7]
    n = len(segs)
    w_refs = refs[7:7 + n]
    out_refs = refs[7 + n:]
    x = x_ref[0].astype(BF16)
    oi = 0
    for (rope, outs), w_ref in zip(segs, w_refs):
        y = jnp.dot(x, w_ref[...], preferred_element_type=F32)
        if rope == "A":
            y = _rope(y, tabs[0][...], tabs[1][...], tabs[2][...], DH_A // 8)
        elif rope == "B":
            y = _rope(y, tabs[3][...], tabs[4][...], tabs[5][...], DH_B // 8)
        for (_, width, scale) in outs:
            v = y[:, :width]
            if scale != 1.0:
                v = v * scale
            out_refs[oi][0] = v.astype(out_refs[oi].dtype)
            oi += 1


def _project(x, tabs, segs, tm, name):
    B, R, D = x.shape
    tm = min(tm, R)
    nblk = pl.cdiv(R, tm)
    in_specs = [pl.BlockSpec((1, tm, D), lambda b, i: (b, i, 0))]
    in_specs += [pl.BlockSpec((tm, LANES), lambda b, i: (i, 0)) for _ in tabs]
    in_specs += [_full_spec(w) for (w, _, _) in segs]
    out_shape, out_specs = [], []
    for (_, _, outs) in segs:
        for (dt, width, _) in outs:
            out_shape.append(jax.ShapeDtypeStruct((B, R, width), dt))
            out_specs.append(pl.BlockSpec((1, tm, width), lambda b, i: (b, i, 0)))
    body = functools.partial(_proj_body, segs=tuple((rope, tuple(outs)) for (_, rope, outs) in segs))
    return pl.pallas_call(
        body, grid=(B, nblk), in_specs=in_specs, out_specs=out_specs, out_shape=out_shape,
        compiler_params=_cparams(("parallel", "parallel")), name=name,
    )(x, *tabs, *[w for (w, _, _) in segs])


def _softmax_step(s, v, m, l, acc):
    m_new = jnp.maximum(m, jnp.max(s, axis=1, keepdims=True))
    alpha = jnp.exp(m - m_new)
    p = jnp.exp(s - m_new)
    l = alpha * l + jnp.sum(p, axis=1, keepdims=True)
    acc = alpha * acc + jnp.dot(p.astype(BF16), v, preferred_element_type=F32)
    return m_new, l, acc


def _diff_lambda(lq1, lk1, lq2, lk2, lam_init):
    a = jnp.sum(lq1[...] * lk1[...], axis=1, keepdims=True)
    b = jnp.sum(lq2[...] * lk2[...], axis=1, keepdims=True)
    return jnp.exp(a) - jnp.exp(b) + lam_init


def _diff_finish(o, gain, lam_init):
    o = o * lax.rsqrt(jnp.mean(o * o, axis=1, keepdims=True) + RMS_EPS) * gain * (1.0 - lam_init)
    return o


def _chunk_start(c):
    return pl.multiple_of(N_META + (c - 1) * KEY_CHUNK, 16)


def _diff_prompt_body(q_ref, k_ref, v_ref, lq1, lk1, lq2, lk2, g_ref, o_ref, *, lam_init, tq):
    i = pl.program_id(2)
    q = q_ref[0].astype(F32)
    lane = lax.broadcasted_iota(I32, (tq, LANES), 1)
    row = lax.broadcasted_iota(I32, (tq, LANES), 0)
    qq = jnp.concatenate([jnp.where(lane < DH_A, q, 0.0), jnp.where(lane >= DH_A, q, 0.0)], axis=0).astype(BF16)
    lane2 = jnp.concatenate([lane, lane], axis=0)
    row2 = jnp.concatenate([row, row], axis=0)

    def step(start, mask, carry):
        kc = k_ref[0, pl.ds(start, KEY_CHUNK), :]
        vc = v_ref[0, pl.ds(start, KEY_CHUNK), :]
        s = lax.dot_general(qq, kc, NT_DIMS, preferred_element_type=F32)
        if mask is not None:
            s = jnp.where(mask, s, -jnp.inf)
        return _softmax_step(s, vc, *carry)

    carry = (jnp.full((2 * tq, 1), NEG, F32), jnp.zeros((2 * tq, 1), F32), jnp.zeros((2 * tq, LANES), F32))
    carry = step(0, lane2 < N_META, carry)
    carry = lax.fori_loop(1, i + 1, lambda c, cr: step(_chunk_start(c), None, cr), carry)
    _, l, acc = step(_chunk_start(i + 1), lane2 <= row2, carry)

    lam = _diff_lambda(lq1, lk1, lq2, lk2, lam_init)
    o = acc[:tq] / l[:tq] - lam * (acc[tq:] / l[tq:])
    o_ref[0] = _diff_finish(o, g_ref[...], lam_init).astype(o_ref.dtype)


def _diff_prompt(qa, ka, va, lam_vecs, gain, lam_init, tq):
    B, S, _ = qa.shape
    T = ka.shape[1]
    nq = S // tq
    body = functools.partial(_diff_prompt_body, lam_init=lam_init, tq=tq)
    res = lambda b, h, i: (b, 0, h)
    blk = lambda b, h, i: (b, i, h)
    return pl.pallas_call(
        body, grid=(B, H_A, nq),
        in_specs=[pl.BlockSpec((1, tq, LANES), blk),
                  pl.BlockSpec((1, T, LANES), res), pl.BlockSpec((1, T, LANES), res)]
                 + [_full_spec(v) for v in lam_vecs] + [_full_spec(gain)],
        out_specs=pl.BlockSpec((1, tq, LANES), blk),
        out_shape=jax.ShapeDtypeStruct((B, S, W_A_DIM), BF16),
        compiler_params=_cparams(("parallel", "parallel", "arbitrary")), name="diff_prompt",
    )(qa, ka, va, *lam_vecs, gain)


def _sort_key(x):
    b = lax.bitcast_convert_type(x, I32)
    b = jnp.where(b == INT_MIN, 0, b)
    return b ^ ((b >> 31) & 0x7FFFFFFF)


def _count_rows(keys_ref, nch, pred):
    rows = keys_ref.shape[0]

    def chunk(c, acc):
        kc = keys_ref[:, pl.ds(pl.multiple_of(c * LANES, LANES), LANES)]
        return acc + jnp.where(pred(kc, c), 1, 0).astype(I32)

    acc = lax.fori_loop(0, nch, chunk, jnp.zeros((rows, LANES), I32))
    return jnp.sum(acc.astype(F32), axis=1, keepdims=True)


def _select_threshold(keys_ref, nch, k):
    rows = keys_ref.shape[0]

    def bit_body(t, prefix):
        cand = prefix ^ lax.shift_left(jnp.int32(1), (31 - t).astype(I32))
        cnt = _count_rows(keys_ref, nch, lambda kc, c: kc >= cand)
        return jnp.where(cnt >= k, cand, prefix)

    tau = lax.fori_loop(0, 32, bit_body, jnp.full((rows, 1), INT_MIN, I32))
    tau = jnp.maximum(tau, INT_MIN + 1)
    n_ge = _count_rows(keys_ref, nch, lambda kc, c: kc >= tau)
    surplus = jnp.max(jnp.maximum(n_ge - k, 0))

    @pl.when(surplus > 0)
    def _():
        n_gt = _count_rows(keys_ref, nch, lambda kc, c: kc > tau)
        allowed = k - n_gt
        lane = lax.broadcasted_iota(I32, (rows, LANES), 1)
        nbits = max(1, int(math.ceil(math.log2(keys_ref.shape[1] + 1))))

        def col_body(t, lim):
            cand = lim | lax.shift_left(jnp.int32(1), (nbits - 1 - t).astype(I32))
            cnt = _count_rows(keys_ref, nch, lambda kc, c: (kc == tau) & (lane + c * LANES < cand))
            return jnp.where(cnt < allowed, cand, lim)

        lim = lax.fori_loop(0, nbits, col_body, jnp.zeros((rows, 1), I32))

        def drop(c, carry):
            sl = pl.ds(pl.multiple_of(c * LANES, LANES), LANES)
            kc = keys_ref[:, sl]
            keys_ref[:, sl] = jnp.where((kc == tau) & (lane + c * LANES > lim), INT_MIN, kc)
            return carry

        lax.fori_loop(0, nch, drop, 0)

    return tau


def _dsa_prompt_body(qb_ref, qi_ref, wi_ref, ki_ref, kb_ref, vb_ref, o_ref, keys_ref, *, tq, topk):
    i = pl.program_id(1)
    nch = i + 2
    lane = lax.broadcasted_iota(I32, (tq, LANES), 1)
    row = lax.broadcasted_iota(I32, (tq, LANES), 0)

    qi = qi_ref[0]
    wi = wi_ref[0]
    qh = [qi[:, h * D_I:(h + 1) * D_I] for h in range(H_I)]
    wh = [jnp.broadcast_to(wi[:, h:h + 1], (tq, LANES)) for h in range(H_I)]

    def score_chunk(c, start, valid):
        kc = ki_ref[0, pl.ds(start, KEY_CHUNK), :]
        sc = jnp.zeros((tq, LANES), F32)
        for h in range(H_I):
            r = lax.dot_general(qh[h], kc, NT_DIMS, preferred_element_type=F32)
            sc = sc + wh[h] * jnp.maximum(r, 0.0)
        key = _sort_key(sc)
        if valid is not None:
            key = jnp.where(valid, key, INT_MIN)
        keys_ref[:, pl.ds(pl.multiple_of(c * LANES, LANES), LANES)] = key

    score_chunk(0, 0, lane < N_META)

    def score_loop(c, carry):
        score_chunk(c, _chunk_start(c), None)
        return carry

    lax.fori_loop(1, i + 1, score_loop, 0)
    score_chunk(i + 1, _chunk_start(i + 1), lane <= row)

    tau = _select_threshold(keys_ref, nch, topk)

    q = qb_ref[0]
    qs = [q[:, h * DH_B:(h + 1) * DH_B] for h in range(H_B)]
    scale = DH_B ** -0.5

    def att_chunk(c, start, carry):
        sel = keys_ref[:, pl.ds(pl.multiple_of(c * LANES, LANES), LANES)] >= tau
        kc = kb_ref[0, pl.ds(start, KEY_CHUNK), :]
        vc = vb_ref[0, pl.ds(start, KEY_CHUNK), :]
        out = []
        for h in range(H_B):
            s = lax.dot_general(qs[h], kc[:, h * DH_B:(h + 1) * DH_B], NT_DIMS, preferred_element_type=F32) * scale
            s = jnp.where(sel, s, -jnp.inf)
            out.append(_softmax_step(s, vc[:, h * DH_B:(h + 1) * DH_B], *carry[h]))
        return tuple(out)

    init = tuple((jnp.full((tq, 1), NEG, F32), jnp.zeros((tq, 1), F32), jnp.zeros((tq, DH_B), F32))
                 for _ in range(H_B))
    carry = att_chunk(0, 0, init)
    carry = lax.fori_loop(1, i + 2, lambda c, cr: att_chunk(c, _chunk_start(c), cr), carry)
    for h in range(H_B):
        _, l, acc = carry[h]
        o_ref[0, :, h * DH_B:(h + 1) * DH_B] = (acc / l).astype(o_ref.dtype)


def _dsa_prompt(qb, qi, wi, ki, kb, vb, topk, tq):
    B, S, _ = qb.shape
    T = kb.shape[1]
    nq = S // tq
    body = functools.partial(_dsa_prompt_body, tq=tq, topk=topk)
    blk = lambda b, i: (b, i, 0)
    res = lambda b, i: (b, 0, 0)
    once = pl.Buffered(1)
    return pl.pallas_call(
        body, grid=(B, nq),
        in_specs=[pl.BlockSpec((1, tq, W_B_DIM), blk), pl.BlockSpec((1, tq, H_I * D_I), blk),
                  pl.BlockSpec((1, tq, H_I), blk),
                  pl.BlockSpec((1, T, D_I), res, pipeline_mode=once),
                  pl.BlockSpec((1, T, W_B_DIM), res, pipeline_mode=once),
                  pl.BlockSpec((1, T, W_B_DIM), res, pipeline_mode=once)],
        out_specs=pl.BlockSpec((1, tq, W_B_DIM), blk),
        out_shape=jax.ShapeDtypeStruct((B, S, W_B_DIM), BF16),
        scratch_shapes=[pltpu.VMEM((tq, (nq + 1) * LANES), I32)],
        compiler_params=_cparams(("parallel", "arbitrary")), name="dsa_prompt",
    )(qb, qi, wi, ki, kb, vb)


def _pad_rows(a, rows):
    return jnp.concatenate([a, jnp.zeros((rows - a.shape[0], a.shape[1]), a.dtype)], axis=0)


def _page_specs(n_pages, width):
    return [pl.BlockSpec((1, KEY_CHUNK, width), functools.partial(lambda b, pt, p: (pt[b, p], 0, 0), p=p))
            for p in range(n_pages)]


def _diff_sample_body(pt_ref, q_ref, kn_ref, vn_ref, *rest, n_pages, lam_init, ts):
    k_pages = rest[:n_pages]
    v_pages = rest[n_pages:2 * n_pages]
    lq1, lk1, lq2, lk2, g_ref, o_ref = rest[2 * n_pages:]
    lam = _diff_lambda(lq1, lk1, lq2, lk2, lam_init)
    q = q_ref[0].astype(F32)
    kn = _pad_rows(kn_ref[0], KEY_CHUNK).astype(BF16)
    vn = _pad_rows(vn_ref[0], KEY_CHUNK).astype(BF16)
    lane = lax.broadcasted_iota(I32, (ts, LANES), 1)
    lane2 = lax.broadcasted_iota(I32, (2 * ts, LANES), 1)
    tok2 = lax.broadcasted_iota(I32, (2 * ts, LANES), 0) % ts
    causal = (lane2 <= tok2) & (lane2 < ts)
    for h in range(H_A):
        hs = slice(h * LANES, (h + 1) * LANES)
        qh = q[:, hs]
        qq = jnp.concatenate([jnp.where(lane < DH_A, qh, 0.0), jnp.where(lane >= DH_A, qh, 0.0)],
                             axis=0).astype(BF16)
        s = jnp.concatenate(
            [lax.dot_general(qq, k_pages[p][0, :, hs].astype(BF16), NT_DIMS, preferred_element_type=F32)
             for p in range(n_pages)]
            + [jnp.where(causal, lax.dot_general(qq, kn[:, hs], NT_DIMS, preferred_element_type=F32), -jnp.inf)],
            axis=1)
        m = jnp.max(s, axis=1, keepdims=True)
        pr = jnp.exp(s - m)
        pr = pr / jnp.sum(pr, axis=1, keepdims=True)
        wgt = _pad_rows(pr[:ts] - lam * pr[ts:], 2 * ts).astype(BF16)
        o = jnp.dot(wgt[:, n_pages * KEY_CHUNK:], vn[:, hs], preferred_element_type=F32)
        for p in range(n_pages):
            o = o + jnp.dot(wgt[:, p * KEY_CHUNK:(p + 1) * KEY_CHUNK], v_pages[p][0, :, hs].astype(BF16),
                            preferred_element_type=F32)
        o_ref[0, :, hs] = _diff_finish(o[:ts], g_ref[...], lam_init).astype(o_ref.dtype)


def _diff_sample(qa, ka_new, va_new, pool_k, pool_v, page_table, lam_vecs, gain, lam_init):
    Bd, ts, _ = qa.shape
    n_pages = page_table.shape[1]
    body = functools.partial(_diff_sample_body, n_pages=n_pages, lam_init=lam_init, ts=ts)
    blk = lambda b, pt: (b, 0, 0)
    const2 = lambda b, pt: (0, 0)
    grid_spec = pltpu.PrefetchScalarGridSpec(
        num_scalar_prefetch=1, grid=(Bd,),
        in_specs=[pl.BlockSpec((1, ts, W_A_DIM), blk)] * 3
                 + _page_specs(n_pages, W_A_DIM) + _page_specs(n_pages, W_A_DIM)
                 + [pl.BlockSpec(v.shape, const2) for v in lam_vecs] + [pl.BlockSpec(gain.shape, const2)],
        out_specs=pl.BlockSpec((1, ts, W_A_DIM), blk))
    return pl.pallas_call(
        body, grid_spec=grid_spec, out_shape=jax.ShapeDtypeStruct((Bd, ts, W_A_DIM), BF16),
        compiler_params=_cparams(("arbitrary",)), name="diff_sample",
    )(page_table, qa, ka_new, va_new, *([pool_k] * n_pages), *([pool_v] * n_pages), *lam_vecs, gain)


def _dsa_sample_body(pt_ref, qb_ref, qi_ref, wi_ref, kin_ref, kn_ref, vn_ref, *rest, n_pages, ts, topk):
    ki_pages = rest[:n_pages]
    k_pages = rest[n_pages:2 * n_pages]
    v_pages = rest[2 * n_pages:3 * n_pages]
    o_ref, keys_ref = rest[3 * n_pages:]
    nch = n_pages + 1
    rows_i = H_I * ts
    qi = qi_ref[0]
    wcol = jnp.broadcast_to(wi_ref[0], (rows_i, LANES))
    lane = lax.broadcasted_iota(I32, (ts, LANES), 1)
    row = lax.broadcasted_iota(I32, (ts, LANES), 0)

    def idx_scores(kc):
        r = lax.dot_general(qi, kc, NT_DIMS, preferred_element_type=F32)
        r = wcol * jnp.maximum(r, 0.0)
        sc = r[0:ts]
        for h in range(1, H_I):
            sc = sc + r[h * ts:(h + 1) * ts]
        return _sort_key(sc)

    for p in range(n_pages):
        keys_ref[:, p * LANES:(p + 1) * LANES] = idx_scores(ki_pages[p][0].astype(BF16))
    kin = _pad_rows(kin_ref[0], KEY_CHUNK).astype(BF16)
    new_valid = (lane <= row) & (lane < ts)
    keys_ref[:, n_pages * LANES:] = jnp.where(new_valid, idx_scores(kin), INT_MIN)

    tau = _select_threshold(keys_ref, nch, topk)

    q = _pad_rows(qb_ref[0].astype(F32), 2 * ts).astype(BF16)
    kn = _pad_rows(kn_ref[0], KEY_CHUNK).astype(BF16)
    vn = _pad_rows(vn_ref[0], KEY_CHUNK).astype(BF16)
    scale = DH_B ** -0.5
    sel = keys_ref[...] >= tau
    for h in range(H_B):
        hs = slice(h * DH_B, (h + 1) * DH_B)
        qh = q[:, hs]
        s = jnp.concatenate(
            [lax.dot_general(qh, k_pages[p][0, :, hs].astype(BF16), NT_DIMS, preferred_element_type=F32)
             for p in range(n_pages)]
            + [lax.dot_general(qh, kn[:, hs], NT_DIMS, preferred_element_type=F32)], axis=1)[:ts] * scale
        s = jnp.where(sel, s, -jnp.inf)
        m = jnp.max(s, axis=1, keepdims=True)
        pr = jnp.exp(s - m)
        pr = _pad_rows(pr / jnp.sum(pr, axis=1, keepdims=True), 2 * ts).astype(BF16)
        o = jnp.dot(pr[:, n_pages * KEY_CHUNK:], vn[:, hs], preferred_element_type=F32)
        for p in range(n_pages):
            o = o + jnp.dot(pr[:, p * KEY_CHUNK:(p + 1) * KEY_CHUNK], v_pages[p][0, :, hs].astype(BF16),
                            preferred_element_type=F32)
        o_ref[0, :, hs] = o[:ts].astype(o_ref.dtype)


def _dsa_sample(qb, qi_hs, wi_hs, ki_new, kb_new, vb_new, pool_ki, pool_k, pool_v, page_table, topk):
    Bd, ts, _ = qb.shape
    n_pages = page_table.shape[1]
    body = functools.partial(_dsa_sample_body, n_pages=n_pages, ts=ts, topk=topk)
    blk = lambda b, pt: (b, 0, 0)
    grid_spec = pltpu.PrefetchScalarGridSpec(
        num_scalar_prefetch=1, grid=(Bd,),
        in_specs=[pl.BlockSpec((1, ts, W_B_DIM), blk), pl.BlockSpec((1, H_I * ts, D_I), blk),
                  pl.BlockSpec((1, H_I * ts, 1), blk), pl.BlockSpec((1, ts, D_I), blk),
                  pl.BlockSpec((1, ts, W_B_DIM), blk), pl.BlockSpec((1, ts, W_B_DIM), blk)]
                 + _page_specs(n_pages, D_I) + _page_specs(n_pages, W_B_DIM) + _page_specs(n_pages, W_B_DIM),
        out_specs=pl.BlockSpec((1, ts, W_B_DIM), blk),
        scratch_shapes=[pltpu.VMEM((ts, (n_pages + 1) * LANES), I32)])
    return pl.pallas_call(
        body, grid_spec=grid_spec, out_shape=jax.ShapeDtypeStruct((Bd, ts, W_B_DIM), BF16),
        compiler_params=_cparams(("arbitrary",)), name="dsa_sample",
    )(page_table, qb, qi_hs, wi_hs, ki_new, kb_new, vb_new,
      *([pool_ki] * n_pages), *([pool_k] * n_pages), *([pool_v] * n_pages))


def _layer_norm(h, g, b):
    mu = jnp.mean(h, axis=1, keepdims=True)
    d = h - mu
    var = jnp.mean(d * d, axis=1, keepdims=True)
    return d * lax.rsqrt(var + LN_EPS) * g + b


def _sigmoid(x):
    return 1.0 / (1.0 + jnp.exp(-x))


def _route(logits):
    tm = logits.shape[0]
    lane_i = lax.broadcasted_iota(I32, (tm, LANES), 1)
    lane = lane_i.astype(F32)
    big = float(LANES)
    is_grp = (lane_i >= N_EXPERTS) & (lane_i < N_EXPERTS + N_GROUPS)
    lg = jnp.where(is_grp, logits, -jnp.inf)
    gmax = jnp.max(lg, axis=1, keepdims=True)
    g_val = 1.0 / jnp.sum(jnp.exp(lg - gmax), axis=1, keepdims=True)
    g_idx = jnp.min(jnp.where(lg == gmax, lane - N_EXPERTS, big), axis=1, keepdims=True)
    grp_of_lane = (lane_i >> int(math.log2(EXP_PER_GROUP))).astype(F32)
    in_grp = (lane_i < N_EXPERTS) & (grp_of_lane == g_idx)
    le = jnp.where(in_grp, logits, -jnp.inf)
    m1 = jnp.max(le, axis=1, keepdims=True)
    i1 = jnp.min(jnp.where(le == m1, lane, big), axis=1, keepdims=True)
    le2 = jnp.where(lane == i1, -jnp.inf, le)
    m2 = jnp.max(le2, axis=1, keepdims=True)
    i2 = jnp.min(jnp.where(le2 == m2, lane, big), axis=1, keepdims=True)
    e2 = jnp.exp(m2 - m1)
    gate1 = g_val / (1.0 + e2)
    gate2 = g_val * e2 / (1.0 + e2)
    return jnp.where(lane == i1, gate1, 0.0) + jnp.where(lane == i2, gate2, 0.0)


def _merge_body(x_ref, oa_ref, ob_ref, wa_ref, wb_ref, wg_ref, wo_ref, g1_ref, b1_ref, wr_ref, br_ref,
                x1_ref, comb_ref, *, alpha):
    x = x_ref[...]
    d = x.shape[1]
    ya = jnp.dot(oa_ref[...], wa_ref[...], preferred_element_type=F32)
    yb = jnp.dot(ob_ref[...], wb_ref[...], preferred_element_type=F32)
    gates = jnp.dot(x.astype(BF16), wg_ref[...], preferred_element_type=F32)
    m = _sigmoid(gates[:, :d]) * ya + _sigmoid(gates[:, d:]) * yb
    mix = jnp.dot(m.astype(BF16), wo_ref[...], preferred_element_type=F32)
    x1 = _layer_norm(alpha * x + mix, g1_ref[...], b1_ref[...])
    x1_ref[...] = x1
    logits = jnp.dot(x1, wr_ref[...], preferred_element_type=F32, precision=lax.Precision.HIGHEST) + br_ref[...]
    comb_ref[...] = _route(logits)


def _merge(x, oa, ob, wa, wb, wg, wo, g1, b1, wr, br, alpha, tm):
    n, d = x.shape
    body = functools.partial(_merge_body, alpha=alpha)
    rowblk = lambda w: pl.BlockSpec((tm, w), lambda i: (i, 0))
    consts = [wa, wb, wg, wo, g1, b1, wr, br]
    return pl.pallas_call(
        body, grid=(n // tm,),
        in_specs=[rowblk(d), rowblk(oa.shape[1]), rowblk(ob.shape[1])] + [_full_spec(c) for c in consts],
        out_specs=[rowblk(d), rowblk(LANES)],
        out_shape=[jax.ShapeDtypeStruct((n, d), F32), jax.ShapeDtypeStruct((n, LANES), F32)],
        compiler_params=_cparams(("parallel",)), name="merge_ln1_route",
    )(x, oa, ob, *consts)


def _moe_body(x_ref, comb_ref, wg_ref, wu_ref, wd_ref, g2_ref, b2_ref, y_ref, xb_ref, acc_ref, *, alpha):
    e = pl.program_id(1)

    @pl.when(e == 0)
    def _():
        xb_ref[...] = x_ref[...].astype(BF16)
        acc_ref[...] = jnp.zeros_like(acc_ref)

    xb = xb_ref[...]
    hg = jnp.dot(xb, wg_ref[0], preferred_element_type=F32)
    hu = jnp.dot(xb, wu_ref[0], preferred_element_type=F32)
    comb = comb_ref[...]
    lane = lax.broadcasted_iota(I32, comb.shape, 1)
    ce = jnp.sum(jnp.where(lane == e, comb, 0.0), axis=1, keepdims=True)
    act = hg * _sigmoid(hg) * hu * ce
    acc_ref[...] += jnp.dot(act.astype(BF16), wd_ref[0], preferred_element_type=F32)

    @pl.when(e == pl.num_programs(1) - 1)
    def _():
        y_ref[...] = _layer_norm(alpha * x_ref[...] + acc_ref[...], g2_ref[...], b2_ref[...])


def _moe(x1, comb, wg, wu, wd, g2, b2, alpha, tm):
    n, d = x1.shape
    ne, _, dff = wg.shape
    body = functools.partial(_moe_body, alpha=alpha)
    return pl.pallas_call(
        body, grid=(n // tm, ne),
        in_specs=[pl.BlockSpec((tm, d), lambda i, e: (i, 0)), pl.BlockSpec((tm, LANES), lambda i, e: (i, 0)),
                  pl.BlockSpec((1, d, dff), lambda i, e: (e, 0, 0)), pl.BlockSpec((1, d, dff), lambda i, e: (e, 0, 0)),
                  pl.BlockSpec((1, dff, d), lambda i, e: (e, 0, 0)),
                  pl.BlockSpec(g2.shape, lambda i, e: (0, 0)), pl.BlockSpec(b2.shape, lambda i, e: (0, 0))],
        out_specs=pl.BlockSpec((tm, d), lambda i, e: (i, 0)),
        out_shape=jax.ShapeDtypeStruct((n, d), F32),
        scratch_shapes=[pltpu.VMEM((tm, d), BF16), pltpu.VMEM((tm, d), F32)],
        compiler_params=_cparams(("parallel", "arbitrary")), name="moe_ln2",
    )(x1, comb, wg, wu, wd, g2, b2)


def _row_tile(n, target):
    t = min(n, target)
    while n % t:
        t //= 2
    return t


def kernel(x_prompt, x_sample, cache_diff_k, cache_diff_v, cache_dsa_k, cache_dsa_v, cache_idx_k, page_table, meta_tokens, w_in, lam_q1, lam_k1, lam_q2, lam_k2, diff_norm_g, w_branch_a, w_branch_b, w_out, ln1_g, ln1_b, w_grp, b_grp, w_exp, b_exp, w_gate, w_up, w_down, ln2_g, ln2_b):
    depth = w_in.shape[0]
    assert depth == 1, "single-layer step only"
    B, S, D = x_prompt.shape
    Bd, Ts, _ = x_sample.shape
    T = S + N_META
    n_pool, page = cache_diff_k.shape[1], cache_diff_k.shape[2]
    assert page == KEY_CHUNK and S % KEY_CHUNK == 0 and T >= KEY_CHUNK
    n_pages = page_table.shape[1]
    past = n_pages * page
    topk_p = min(TOPK_MAX, S // 4)
    topk_s = min(TOPK_MAX, (past + Ts) // 4)
    alpha = (2.0 * depth) ** 0.25
    lam_init = 0.8 - 0.6 * math.exp(-0.3 * 0)

    w = w_in[0]
    widths = list(IN_WIDTHS[:9]) + [D, D]
    cuts = np.concatenate([[0], np.cumsum(widths)])
    w_qa, w_ka, w_va, w_qb, w_kb, w_vb, w_qi, w_ki, w_wi, w_ga, w_gb = [
        w[:, int(cuts[j]):int(cuts[j + 1])].astype(BF16) for j in range(11)]
    w_ki = jnp.pad(w_ki, ((0, 0), (0, LANES - D_I)))
    w_wi = jnp.pad(w_wi, ((0, 0), (0, LANES - H_I)))
    w_g = jnp.concatenate([w_ga, w_gb], axis=1)
    lam_vecs = [v.astype(F32) for v in (lam_q1, lam_k1, lam_q2, lam_k2)]
    gain = diff_norm_g.astype(F32)
    w_r = jnp.pad(jnp.concatenate([w_exp[0], w_grp[0]], axis=1), ((0, 0), (0, LANES - N_EXPERTS - N_GROUPS)))
    b_r = jnp.pad(jnp.concatenate([b_exp[0], b_grp[0]])[None, :], ((0, 0), (0, LANES - N_EXPERTS - N_GROUPS)))
    wa16, wb16, wo16 = w_branch_a[0].astype(BF16), w_branch_b[0].astype(BF16), w_out[0].astype(BF16)
    wg16, wu16, wd16 = w_gate[0].astype(BF16), w_up[0].astype(BF16), w_down[0].astype(BF16)

    k_segs = [(w_ka, "A", [(F32, W_A_DIM, 1.0), (BF16, W_A_DIM, 1.0)]),
              (w_va, None, [(F32, W_A_DIM, 1.0), (BF16, W_A_DIM, 1.0)]),
              (w_kb, "B", [(F32, W_B_DIM, 1.0), (BF16, W_B_DIM, 1.0)]),
              (w_vb, None, [(F32, W_B_DIM, 1.0), (BF16, W_B_DIM, 1.0)]),
              (w_ki, "A", [(F32, D_I, 1.0), (BF16, D_I, 1.0)])]
    q_segs = [(w_qa, "A", [(BF16, W_A_DIM, DH_A ** -0.5)]),
              (w_qb, "B", [(BF16, W_B_DIM, 1.0)]),
              (w_qi, "A", [(BF16, H_I * D_I, D_I ** -0.5)]),
              (w_wi, None, [(F32, H_I, H_I ** -0.5)])]

    pos_p = jnp.arange(T, dtype=jnp.int32)
    tabs_p = _rope_tables(pos_p, DH_A // 4, DH_A) + _rope_tables(pos_p, DH_B // 4, DH_B)
    tabs_q = tuple(t[N_META:] for t in tabs_p)
    meta = jnp.broadcast_to(meta_tokens[None].astype(x_prompt.dtype), (B, N_META, D))
    xp = jnp.concatenate([meta, x_prompt], axis=1)
    (ka_p, ka16, va_p, va16, kb_p, kb16, vb_p, vb16, ki_p, ki16) = _project(xp, tabs_p, k_segs, 512, "proj_k_prompt")
    qa16, qb16, qi16, wi_p = _project(x_prompt, tabs_q, q_segs, _row_tile(S, 512), "proj_q_prompt")
    oa_p = _diff_prompt(qa16, ka16, va16, lam_vecs, gain, lam_init, KEY_CHUNK)
    ob_p = _dsa_prompt(qb16, qi16, wi_p, ki16, kb16, vb16, topk_p, KEY_CHUNK)

    ns = Bd * Ts
    pos_s = past + (jnp.arange(ns, dtype=jnp.int32) % Ts)
    tabs_s = _rope_tables(pos_s, DH_A // 4, DH_A) + _rope_tables(pos_s, DH_B // 4, DH_B)
    s_segs = [(wt, rope, [o for o in outs if o[0] == F32]) for (wt, rope, outs) in k_segs] + q_segs
    (ka_s, va_s, kb_s, vb_s, ki_s, qa_s, qb_s, qi_s, wi_s) = _project(
        x_sample.reshape(1, ns, D), tabs_s, s_segs, _row_tile(ns, 512), "proj_sample")
    seq = lambda a: a.reshape(Bd, Ts, a.shape[-1])
    pool = lambda c: c.reshape(n_pool, page, -1)
    oa_s = _diff_sample(seq(qa_s), seq(ka_s), seq(va_s), pool(cache_diff_k), pool(cache_diff_v), page_table,
                        lam_vecs, gain, lam_init)
    qi_hs = seq(qi_s).reshape(Bd, Ts, H_I, D_I).transpose(0, 2, 1, 3).reshape(Bd, H_I * Ts, D_I)
    wi_hs = seq(wi_s).transpose(0, 2, 1).reshape(Bd, H_I * Ts, 1)
    ob_s = _dsa_sample(seq(qb_s), qi_hs, wi_hs, seq(ki_s), seq(kb_s), seq(vb_s), pool(cache_idx_k),
                       pool(cache_dsa_k), pool(cache_dsa_v), page_table, topk_s)

    def tail(x2d, oa2d, ob2d):
        n = x2d.shape[0]
        x1, comb = _merge(x2d, oa2d, ob2d, wa16, wb16, w_g, wo16, ln1_g, ln1_b, w_r, b_r, alpha, _row_tile(n, 256))
        return _moe(x1, comb, wg16, wu16, wd16, ln2_g, ln2_b, alpha, _row_tile(n, 1024))

    y_prompt = tail(x_prompt.reshape(B * S, D), oa_p.reshape(B * S, -1), ob_p.reshape(B * S, -1)).reshape(B, S, D)
    y_sample = tail(x_sample.reshape(ns, D), oa_s.reshape(ns, -1), ob_s.reshape(ns, -1)).reshape(Bd, Ts, D)

    c4 = lambda a, h, dh: a.reshape(1, a.shape[0], a.shape[1], h, dh)
    return (y_prompt, y_sample,
            c4(ka_p, H_A, 2 * DH_A), c4(va_p, H_A, 2 * DH_A), c4(kb_p, H_B, DH_B), c4(vb_p, H_B, DH_B), ki_p[None],
            c4(seq(ka_s), H_A, 2 * DH_A), c4(seq(va_s), H_A, 2 * DH_A), c4(seq(kb_s), H_B, DH_B),
            c4(seq(vb_s), H_B, DH_B), seq(ki_s)[None])
```

```python
import functools
import math

import numpy as np
import jax
import jax.numpy as jnp
from jax import lax
from jax.experimental import pallas as pl
from jax.experimental.pallas import tpu as pltpu

F32 = jnp.float32
BF16 = jnp.bfloat16
I32 = jnp.int32

N_META = 16
H_A = 4
DH_A = 64
H_B = 4
DH_B = 128
H_I = 8
D_I = 64
TOPK_MAX = 256
ROPE_THETA = 500000.0
N_GROUPS = 4
EXP_PER_GROUP = 8
N_EXPERTS = N_GROUPS * EXP_PER_GROUP
LN_EPS = 1e-5
RMS_EPS = 1e-6
W_A_DIM = H_A * 2 * DH_A
W_B_DIM = H_B * DH_B
IN_WIDTHS = (W_A_DIM, W_A_DIM, W_A_DIM, W_B_DIM, W_B_DIM, W_B_DIM, H_I * D_I, D_I, H_I, None, None)

LANES = 128
KEY_CHUNK = 128
NEG = -1e30
INT_MIN = -(2 ** 31)
VMEM_LIMIT = 56 * 1024 * 1024

NT_DIMS = (((1,), (1,)), ((), ()))


def _cparams(sem):
    return pltpu.CompilerParams(dimension_semantics=sem, vmem_limit_bytes=VMEM_LIMIT)


def _full_spec(a):
    nd = a.ndim
    return pl.BlockSpec(a.shape, lambda *_: (0,) * nd)


def _rope_tables(pos, rot, period):
    half = rot // 2
    inv = jnp.power(ROPE_THETA, -jnp.arange(half, dtype=F32) / half)
    ang = pos.astype(F32)[:, None] * inv[None, :]
    cos, sin = jnp.cos(ang), jnp.sin(ang)
    n = pos.shape[0]
    zh = jnp.zeros((n, half), F32)
    rest0 = jnp.zeros((n, period - rot), F32)
    c = jnp.concatenate([cos, cos, jnp.ones((n, period - rot), F32)], axis=1)
    sn = jnp.concatenate([-sin, zh, rest0], axis=1)
    sp = jnp.concatenate([zh, sin, rest0], axis=1)
    reps = LANES // period
    return tuple(jnp.tile(t, (1, reps)) for t in (c, sn, sp))


def _rope(y, c, sn, sp, half):
    outs = []
    for j in range(y.shape[1] // LANES):
        ys = y[:, j * LANES:(j + 1) * LANES]
        outs.append(ys * c + pltpu.roll(ys, LANES - half, 1) * sn + pltpu.roll(ys, half, 1) * sp)
    return outs[0] if len(outs) == 1 else jnp.concatenate(outs, axis=1)


def _proj_body(*refs, segs):
    x_ref = refs[0]
    tabs = refs[1:7]
    n = len(segs)
    w_refs = refs[7:7 + n]
    out_refs = refs[7 + n:]
    x = x_ref[0].astype(BF16)
    oi = 0
    for (rope, outs), w_ref in zip(segs, w_refs):
        y = jnp.dot(x, w_ref[...], preferred_element_type=F32)
        if rope == "A":
            y = _rope(y, tabs[0][...], tabs[1][...], tabs[2][...], DH_A // 8)
        elif rope == "B":
            y = _rope(y, tabs[3][...], tabs[4][...], tabs[5][...], DH_B // 8)
        for (_, width, scale) in outs:
            v = y[:, :width]
            if scale != 1.0:
                v = v * scale
            out_refs[oi][0] = v.astype(out_refs[oi].dtype)
            oi += 1


def _project(x, tabs, segs, tm, name):
    B, R, D = x.shape
    tm = min(tm, R)
    nblk = pl.cdiv(R, tm)
    in_specs = [pl.BlockSpec((1, tm, D), lambda b, i: (b, i, 0))]
    in_specs += [pl.BlockSpec((tm, LANES), lambda b, i: (i, 0)) for _ in tabs]
    in_specs += [_full_spec(w) for (w, _, _) in segs]
    out_shape, out_specs = [], []
    for (_, _, outs) in segs:
        for (dt, width, _) in outs:
            out_shape.append(jax.ShapeDtypeStruct((B, R, width), dt))
            out_specs.append(pl.BlockSpec((1, tm, width), lambda b, i: (b, i, 0)))
    body = functools.partial(_proj_body, segs=tuple((rope, tuple(outs)) for (_, rope, outs) in segs))
    return pl.pallas_call(
        body, grid=(B, nblk), in_specs=in_specs, out_specs=out_specs, out_shape=out_shape,
        compiler_params=_cparams(("parallel", "parallel")), name=name,
    )(x, *tabs, *[w for (w, _, _) in segs])


def _softmax_step(s, v, m, l, acc):
    m_new = jnp.maximum(m, jnp.max(s, axis=1, keepdims=True))
    alpha = jnp.exp(m - m_new)
    p = jnp.exp(s - m_new)
    l = alpha * l + jnp.sum(p, axis=1, keepdims=True)
    acc = alpha * acc + jnp.dot(p.astype(BF16), v, preferred_element_type=F32)
    return m_new, l, acc


def _diff_lambda(lq1, lk1, lq2, lk2, lam_init):
    a = jnp.sum(lq1[...] * lk1[...], axis=1, keepdims=True)
    b = jnp.sum(lq2[...] * lk2[...], axis=1, keepdims=True)
    return jnp.exp(a) - jnp.exp(b) + lam_init


def _diff_finish(o, gain, lam_init):
    o = o * lax.rsqrt(jnp.mean(o * o, axis=1, keepdims=True) + RMS_EPS) * gain * (1.0 - lam_init)
    return o


def _group_start(g, tk):
    return pl.multiple_of(N_META + g * tk, 16)


def _diff_prompt_body(q_ref, k_ref, v_ref, lq1, lk1, lq2, lk2, g_ref, o_ref, *, lam_init, tq, tk):
    i = pl.program_id(2)
    q = q_ref[0].astype(F32)
    lane = lax.broadcasted_iota(I32, (tq, LANES), 1)
    qq = jnp.concatenate([jnp.where(lane < DH_A, q, 0.0), jnp.where(lane >= DH_A, q, 0.0)], axis=0).astype(BF16)

    def step(start, nk, mask, carry):
        kc = k_ref[0, pl.ds(start, nk), :]
        vc = v_ref[0, pl.ds(start, nk), :]
        s = lax.dot_general(qq, kc, NT_DIMS, preferred_element_type=F32)
        if mask is not None:
            s = jnp.where(mask, s, -jnp.inf)
        return _softmax_step(s, vc, *carry)

    carry = (jnp.full((2 * tq, 1), NEG, F32), jnp.zeros((2 * tq, 1), F32), jnp.zeros((2 * tq, LANES), F32))
    carry = step(0, KEY_CHUNK, lax.broadcasted_iota(I32, (2 * tq, KEY_CHUNK), 1) < N_META, carry)
    n_full = (i * tq) // tk
    carry = lax.fori_loop(0, n_full, lambda g, cr: step(_group_start(g, tk), tk, None, cr), carry)
    kx = n_full * tk + lax.broadcasted_iota(I32, (2 * tq, tk), 1)
    qx = i * tq + (lax.broadcasted_iota(I32, (2 * tq, tk), 0) & (tq - 1))
    _, l, acc = step(_group_start(n_full, tk), tk, kx <= qx, carry)

    lam = _diff_lambda(lq1, lk1, lq2, lk2, lam_init)
    o = acc[:tq] / l[:tq] - lam * (acc[tq:] / l[tq:])
    o_ref[0] = _diff_finish(o, g_ref[...], lam_init).astype(o_ref.dtype)


def _diff_prompt(qa, ka, va, lam_vecs, gain, lam_init, tq, tk):
    B, S, _ = qa.shape
    T = ka.shape[1]
    nq = S // tq
    body = functools.partial(_diff_prompt_body, lam_init=lam_init, tq=tq, tk=tk)
    res = lambda b, h, i: (b, 0, h)
    blk = lambda b, h, i: (b, i, h)
    return pl.pallas_call(
        body, grid=(B, H_A, nq),
        in_specs=[pl.BlockSpec((1, tq, LANES), blk),
                  pl.BlockSpec((1, T, LANES), res), pl.BlockSpec((1, T, LANES), res)]
                 + [_full_spec(v) for v in lam_vecs] + [_full_spec(gain)],
        out_specs=pl.BlockSpec((1, tq, LANES), blk),
        out_shape=jax.ShapeDtypeStruct((B, S, W_A_DIM), BF16),
        compiler_params=_cparams(("parallel", "parallel", "arbitrary")), name="diff_prompt",
    )(qa, ka, va, *lam_vecs, gain)


def _sort_key(x):
    b = lax.bitcast_convert_type(x, I32)
    b = jnp.where(b == INT_MIN, 0, b)
    return b ^ ((b >> 31) & 0x7FFFFFFF)


def _count_rows(keys_ref, ngrp, cpg, pred):
    rows = keys_ref.shape[0]

    def one(col0, acc):
        kc = keys_ref[:, pl.ds(col0, LANES)]
        return acc + jnp.where(pred(kc, col0), 1, 0).astype(I32)

    def grp(g, acc):
        base = pl.multiple_of(LANES + g * (cpg * LANES), LANES)
        for j in range(cpg):
            acc = one(base + j * LANES, acc)
        return acc

    acc = lax.fori_loop(0, ngrp, grp, one(0, jnp.zeros((rows, LANES), I32)))
    cnt = jnp.sum(acc.astype(F32), axis=1, keepdims=True)
    return jnp.broadcast_to(cnt, (rows, LANES))


def _select_threshold(keys_ref, ngrp, cpg, k):
    rows = keys_ref.shape[0]

    def bit_body(t, prefix):
        cand = prefix ^ lax.shift_left(jnp.int32(1), (31 - t).astype(I32))
        cnt = _count_rows(keys_ref, ngrp, cpg, lambda kc, c0: kc >= cand)
        return jnp.where(cnt >= k, cand, prefix)

    tau = lax.fori_loop(0, 32, bit_body, jnp.full((rows, LANES), INT_MIN, I32))
    tau = jnp.maximum(tau, INT_MIN + 1)
    n_ge = _count_rows(keys_ref, ngrp, cpg, lambda kc, c0: kc >= tau)
    surplus = jnp.max(jnp.maximum(n_ge - k, 0.0))

    @pl.when(surplus > 0.0)
    def _():
        n_gt = _count_rows(keys_ref, ngrp, cpg, lambda kc, c0: kc > tau)
        allowed = k - n_gt
        lane = lax.broadcasted_iota(I32, (rows, LANES), 1)
        nbits = max(1, int(math.ceil(math.log2(keys_ref.shape[1] + 1))))

        def col_body(t, lim):
            cand = lim | lax.shift_left(jnp.int32(1), (nbits - 1 - t).astype(I32))
            cnt = _count_rows(keys_ref, ngrp, cpg, lambda kc, c0: (kc == tau) & (lane + c0 < cand))
            return jnp.where(cnt < allowed, cand, lim)

        lim = lax.fori_loop(0, nbits, col_body, jnp.zeros((rows, LANES), I32))

        def drop(c, carry):
            c0 = pl.multiple_of(c * LANES, LANES)
            kc = keys_ref[:, pl.ds(c0, LANES)]
            keys_ref[:, pl.ds(c0, LANES)] = jnp.where((kc == tau) & (lane + c0 > lim), INT_MIN, kc)
            return carry

        lax.fori_loop(0, 1 + ngrp * cpg, drop, 0)

    return tau


SCORE_BLOCK = 256


def _masked_softmax_step(s, sel, v, c2, m, l, acc):
    s = jnp.where(sel, s, -jnp.inf)
    m_new = jnp.maximum(m, jnp.max(s, axis=1, keepdims=True))
    alpha = jnp.exp2((m - m_new) * c2)
    p = jnp.exp2((s - m_new) * c2)
    l = alpha * l + jnp.sum(p, axis=1, keepdims=True)
    acc = alpha * acc + jnp.dot(p.astype(BF16), v, preferred_element_type=F32)
    return m_new, l, acc


def _dsa_prompt_body(qb_ref, qi_ref, wi_ref, ki_ref, kb_ref, vb_ref, o_ref, keys_ref, *, tq, tk, topk):
    i = pl.program_id(1)
    n_full = (i * tq) // tk
    ngrp = n_full + 1
    cpg = tk // LANES
    lane = lax.broadcasted_iota(I32, (tq, LANES), 1)
    row = lax.broadcasted_iota(I32, (tq, LANES), 0)

    qi = qi_ref[0]
    wi = wi_ref[0]
    qst = jnp.concatenate([qi[:, h * D_I:(h + 1) * D_I] for h in range(H_I)], axis=0)
    wh = [jnp.broadcast_to(wi[:, h:h + 1], (tq, LANES)) for h in range(H_I)]

    def score(start, nk, col0, x0):
        kc = ki_ref[0, pl.ds(start, nk), :]
        r = lax.dot_general(qst, kc, NT_DIMS, preferred_element_type=F32)
        for j in range(nk // LANES):
            sc = jnp.zeros((tq, LANES), F32)
            for h in range(H_I):
                sc = sc + wh[h] * jnp.maximum(r[h * tq:(h + 1) * tq, j * LANES:(j + 1) * LANES], 0.0)
            key = _sort_key(sc)
            if x0 is None:
                key = jnp.where(lane < N_META, key, INT_MIN)
            elif x0 is not False:
                key = jnp.where(x0 + j * LANES + lane <= i * tq + row, key, INT_MIN)
            keys_ref[:, pl.ds(col0 + j * LANES, LANES)] = key

    score(0, KEY_CHUNK, 0, None)

    def score_group(g, masked):
        sb = min(SCORE_BLOCK, tk)
        for jb in range(tk // sb):
            off = g * tk + jb * sb
            score(pl.multiple_of(N_META + off, 16), sb, pl.multiple_of(LANES + off, LANES),
                  off if masked else False)

    def score_loop(g, carry):
        score_group(g, False)
        return carry

    lax.fori_loop(0, n_full, score_loop, 0)
    score_group(n_full, True)

    tau = _select_threshold(keys_ref, ngrp, cpg, topk)

    q = qb_ref[0]
    qs = [q[:, h * DH_B:(h + 1) * DH_B] for h in range(H_B)]
    c2 = (DH_B ** -0.5) * math.log2(math.e)

    def att(start, nk, col0, carry):
        taub = tau if nk == LANES else jnp.concatenate([tau] * (nk // LANES), axis=1)
        sel = keys_ref[:, pl.ds(col0, nk)] >= taub
        kc = kb_ref[0, pl.ds(start, nk), :]
        vc = vb_ref[0, pl.ds(start, nk), :]
        out = []
        for h in range(H_B):
            hs = slice(h * DH_B, (h + 1) * DH_B)
            s = lax.dot_general(qs[h], kc[:, hs], NT_DIMS, preferred_element_type=F32)
            out.append(_masked_softmax_step(s, sel, vc[:, hs], c2, *carry[h]))
        return tuple(out)

    init = tuple((jnp.full((tq, 1), NEG, F32), jnp.zeros((tq, 1), F32), jnp.zeros((tq, DH_B), F32))
                 for _ in range(H_B))
    carry = att(0, KEY_CHUNK, 0, init)
    carry = lax.fori_loop(
        0, ngrp, lambda g, cr: att(_group_start(g, tk), tk, pl.multiple_of(LANES + g * tk, LANES), cr), carry)
    for h in range(H_B):
        _, l, acc = carry[h]
        o_ref[0, :, h * DH_B:(h + 1) * DH_B] = (acc / l).astype(o_ref.dtype)


def _dsa_prompt(qb, qi, wi, ki, kb, vb, topk, tq, tk):
    B, S, _ = qb.shape
    T = kb.shape[1]
    nq = S // tq
    body = functools.partial(_dsa_prompt_body, tq=tq, tk=tk, topk=topk)
    blk = lambda b, i: (b, i, 0)
    res = lambda b, i: (b, 0, 0)
    once = pl.Buffered(1)
    return pl.pallas_call(
        body, grid=(B, nq),
        in_specs=[pl.BlockSpec((1, tq, W_B_DIM), blk), pl.BlockSpec((1, tq, H_I * D_I), blk),
                  pl.BlockSpec((1, tq, H_I), blk),
                  pl.BlockSpec((1, T, D_I), res, pipeline_mode=once),
                  pl.BlockSpec((1, T, W_B_DIM), res, pipeline_mode=once),
                  pl.BlockSpec((1, T, W_B_DIM), res, pipeline_mode=once)],
        out_specs=pl.BlockSpec((1, tq, W_B_DIM), blk),
        out_shape=jax.ShapeDtypeStruct((B, S, W_B_DIM), BF16),
        scratch_shapes=[pltpu.VMEM((tq, LANES + S), I32)],
        compiler_params=_cparams(("parallel", "arbitrary")), name="dsa_prompt",
    )(qb, qi, wi, ki, kb, vb)


def _div(x, d):
    assert d & (d - 1) == 0
    return x >> (d.bit_length() - 1)


def _mod(x, d):
    assert d & (d - 1) == 0
    return x & (d - 1)


def _pad_rows(a, rows):
    return jnp.concatenate([a, jnp.zeros((rows - a.shape[0], a.shape[1]), a.dtype)], axis=0)


def _page_specs(n_pages, rows, width):
    return [pl.BlockSpec((rows, width), functools.partial(lambda b, pt, p: (pt[b, p], 0), p=p))
            for p in range(n_pages)]


def _head_rows(q, heads):
    return jnp.concatenate([q[:, h * LANES:(h + 1) * LANES] for h in range(heads)], axis=0)


def _diff_sample_body(pt_ref, q_ref, kn_ref, vn_ref, *rest, n_pages, lam_init, ts):
    k_pages = rest[:n_pages]
    v_pages = rest[n_pages:2 * n_pages]
    lq1, lk1, lq2, lk2, g_ref, o_ref = rest[2 * n_pages:]
    lam = _diff_lambda(lq1, lk1, lq2, lk2, lam_init)
    rows = H_A * ts
    pw = KEY_CHUNK * H_A
    qh = _head_rows(q_ref[0].astype(F32), H_A)
    lane = lax.broadcasted_iota(I32, (rows, LANES), 1)
    qq = jnp.concatenate([jnp.where(lane < DH_A, qh, 0.0), jnp.where(lane >= DH_A, qh, 0.0)], axis=0).astype(BF16)
    kn = _pad_rows(kn_ref[0], LANES).astype(BF16)
    vn = _pad_rows(vn_ref[0], LANES).astype(BF16)
    r2 = lax.broadcasted_iota(I32, (2 * rows, pw), 0)
    c2 = lax.broadcasted_iota(I32, (2 * rows, pw), 1)
    same_head = _mod(_div(r2, ts), H_A) == _mod(c2, H_A)
    rn = lax.broadcasted_iota(I32, (2 * rows, LANES), 0)
    cn = lax.broadcasted_iota(I32, (2 * rows, LANES), 1)
    new_ok = (_mod(_div(rn, ts), H_A) == _mod(cn, H_A)) & (_div(cn, H_A) <= _mod(rn, ts)) & (cn < ts * H_A)
    s = jnp.concatenate(
        [jnp.where(same_head, lax.dot_general(qq, k_pages[p][...].astype(BF16), NT_DIMS,
                                              preferred_element_type=F32), -jnp.inf) for p in range(n_pages)]
        + [jnp.where(new_ok, lax.dot_general(qq, kn, NT_DIMS, preferred_element_type=F32), -jnp.inf)], axis=1)
    m = jnp.max(s, axis=1, keepdims=True)
    pr = jnp.exp(s - m)
    pr = pr / jnp.sum(pr, axis=1, keepdims=True)
    wgt = (pr[:rows] - lam * pr[rows:]).astype(BF16)
    o = jnp.dot(wgt[:, n_pages * pw:], vn, preferred_element_type=F32)
    for p in range(n_pages):
        o = o + jnp.dot(wgt[:, p * pw:(p + 1) * pw], v_pages[p][...].astype(BF16), preferred_element_type=F32)
    o = _diff_finish(o, g_ref[...], lam_init)
    for h in range(H_A):
        o_ref[0, :, h * LANES:(h + 1) * LANES] = o[h * ts:(h + 1) * ts].astype(o_ref.dtype)


def _diff_sample(qa, ka_new, va_new, pool_k, pool_v, page_table, lam_vecs, gain, lam_init):
    Bd, ts, _ = qa.shape
    n_pages = page_table.shape[1]
    pw = KEY_CHUNK * H_A
    body = functools.partial(_diff_sample_body, n_pages=n_pages, lam_init=lam_init, ts=ts)
    blk = lambda b, pt: (b, 0, 0)
    const2 = lambda b, pt: (0, 0)
    new_spec = pl.BlockSpec((1, ts * H_A, LANES), blk)
    grid_spec = pltpu.PrefetchScalarGridSpec(
        num_scalar_prefetch=1, grid=(Bd,),
        in_specs=[pl.BlockSpec((1, ts, W_A_DIM), blk), new_spec, new_spec]
                 + _page_specs(n_pages, pw, LANES) + _page_specs(n_pages, pw, LANES)
                 + [pl.BlockSpec(v.shape, const2) for v in lam_vecs] + [pl.BlockSpec(gain.shape, const2)],
        out_specs=pl.BlockSpec((1, ts, W_A_DIM), blk))
    return pl.pallas_call(
        body, grid_spec=grid_spec, out_shape=jax.ShapeDtypeStruct((Bd, ts, W_A_DIM), BF16),
        compiler_params=_cparams(("arbitrary",)), name="diff_sample",
    )(page_table, qa, ka_new, va_new, *([pool_k] * n_pages), *([pool_v] * n_pages), *lam_vecs, gain)


def _dsa_sample_body(pt_ref, qb_ref, qi_ref, wi_ref, kin_ref, kn_ref, vn_ref, *rest, n_pages, ts, topk, cpg):
    ki_pages = rest[:n_pages]
    k_pages = rest[n_pages:2 * n_pages]
    v_pages = rest[2 * n_pages:3 * n_pages]
    o_ref, keys_ref = rest[3 * n_pages:]
    rows_i = H_I * ts
    qi = qi_ref[0]
    wcol = jnp.broadcast_to(wi_ref[0], (rows_i, LANES))
    lane = lax.broadcasted_iota(I32, (ts, LANES), 1)
    row = lax.broadcasted_iota(I32, (ts, LANES), 0)

    def idx_scores(kc):
        r = lax.dot_general(qi, kc, NT_DIMS, preferred_element_type=F32)
        r = wcol * jnp.maximum(r, 0.0)
        sc = r[0:ts]
        for h in range(1, H_I):
            sc = sc + r[h * ts:(h + 1) * ts]
        return _sort_key(sc)

    for p in range(n_pages):
        keys_ref[:, p * LANES:(p + 1) * LANES] = idx_scores(ki_pages[p][0].astype(BF16))
    kin = _pad_rows(kin_ref[0], KEY_CHUNK).astype(BF16)
    keys_ref[:, n_pages * LANES:] = jnp.where((lane <= row) & (lane < ts), idx_scores(kin), INT_MIN)

    tau = _select_threshold(keys_ref, n_pages // cpg, cpg, topk)

    pw = KEY_CHUNK * H_B
    rows = H_B * ts
    expand = jnp.where(_div(lax.broadcasted_iota(I32, (KEY_CHUNK, pw), 1), H_B)
                       == lax.broadcasted_iota(I32, (KEY_CHUNK, pw), 0), 1.0, 0.0).astype(BF16)
    r2 = lax.broadcasted_iota(I32, (rows, pw), 0)
    c2i = lax.broadcasted_iota(I32, (rows, pw), 1)
    same_head = _div(r2, ts) == _mod(c2i, H_B)

    def selected(p):
        sel = jnp.where(keys_ref[:, p * LANES:(p + 1) * LANES] >= tau, 1.0, 0.0)
        sel = jnp.dot(_pad_rows(sel, 2 * ts).astype(BF16), expand, preferred_element_type=F32)[:ts]
        return same_head & (jnp.concatenate([sel] * H_B, axis=0) > 0.5)

    qh = _head_rows(qb_ref[0].astype(F32), H_B).astype(BF16)
    kn = _pad_rows(kn_ref[0], LANES).astype(BF16)
    vn = _pad_rows(vn_ref[0], LANES).astype(BF16)
    c2 = (DH_B ** -0.5) * math.log2(math.e)
    s = jnp.concatenate(
        [jnp.where(selected(p), lax.dot_general(qh, k_pages[p][...].astype(BF16), NT_DIMS,
                                                preferred_element_type=F32), -jnp.inf) for p in range(n_pages)]
        + [jnp.where(selected(n_pages)[:, :LANES], lax.dot_general(qh, kn, NT_DIMS, preferred_element_type=F32),
                     -jnp.inf)], axis=1)
    m = jnp.max(s, axis=1, keepdims=True)
    pr = jnp.exp2((s - m) * c2)
    pr = (pr / jnp.sum(pr, axis=1, keepdims=True)).astype(BF16)
    o = jnp.dot(pr[:, n_pages * pw:], vn, preferred_element_type=F32)
    for p in range(n_pages):
        o = o + jnp.dot(pr[:, p * pw:(p + 1) * pw], v_pages[p][...].astype(BF16), preferred_element_type=F32)
    for h in range(H_B):
        o_ref[0, :, h * LANES:(h + 1) * LANES] = o[h * ts:(h + 1) * ts].astype(o_ref.dtype)


def _dsa_sample(qb, qi_hs, wi_hs, ki_new, kb_new, vb_new, pool_ki, pool_k, pool_v, page_table, topk):
    Bd, ts, _ = qb.shape
    n_pages = page_table.shape[1]
    pw = KEY_CHUNK * H_B
    cpg = 4 if n_pages % 4 == 0 else 1
    body = functools.partial(_dsa_sample_body, n_pages=n_pages, ts=ts, topk=topk, cpg=cpg)
    blk = lambda b, pt: (b, 0, 0)
    new_spec = pl.BlockSpec((1, ts * H_B, LANES), blk)
    ki_specs = [pl.BlockSpec((1, KEY_CHUNK, D_I), functools.partial(lambda b, pt, p: (pt[b, p], 0, 0), p=p))
                for p in range(n_pages)]
    grid_spec = pltpu.PrefetchScalarGridSpec(
        num_scalar_prefetch=1, grid=(Bd,),
        in_specs=[pl.BlockSpec((1, ts, W_B_DIM), blk), pl.BlockSpec((1, H_I * ts, D_I), blk),
                  pl.BlockSpec((1, H_I * ts, 1), blk), pl.BlockSpec((1, ts, D_I), blk), new_spec, new_spec]
                 + ki_specs + _page_specs(n_pages, pw, LANES) + _page_specs(n_pages, pw, LANES),
        out_specs=pl.BlockSpec((1, ts, W_B_DIM), blk),
        scratch_shapes=[pltpu.VMEM((ts, (n_pages + 1) * LANES), I32)])
    return pl.pallas_call(
        body, grid_spec=grid_spec, out_shape=jax.ShapeDtypeStruct((Bd, ts, W_B_DIM), BF16),
        compiler_params=_cparams(("arbitrary",)), name="dsa_sample",
    )(page_table, qb, qi_hs, wi_hs, ki_new, kb_new, vb_new,
      *([pool_ki] * n_pages), *([pool_k] * n_pages), *([pool_v] * n_pages))


def _layer_norm(h, g, b):
    mu = jnp.mean(h, axis=1, keepdims=True)
    d = h - mu
    var = jnp.mean(d * d, axis=1, keepdims=True)
    return d * lax.rsqrt(var + LN_EPS) * g + b


def _sigmoid(x):
    return 1.0 / (1.0 + jnp.exp(-x))


def _route(logits):
    tm = logits.shape[0]
    lane_i = lax.broadcasted_iota(I32, (tm, LANES), 1)
    lane = lane_i.astype(F32)
    big = float(LANES)
    is_grp = (lane_i >= N_EXPERTS) & (lane_i < N_EXPERTS + N_GROUPS)
    lg = jnp.where(is_grp, logits, -jnp.inf)
    gmax = jnp.max(lg, axis=1, keepdims=True)
    g_val = 1.0 / jnp.sum(jnp.exp(lg - gmax), axis=1, keepdims=True)
    g_idx = jnp.min(jnp.where(lg == gmax, lane - N_EXPERTS, big), axis=1, keepdims=True)
    grp_of_lane = (lane_i >> int(math.log2(EXP_PER_GROUP))).astype(F32)
    in_grp = (lane_i < N_EXPERTS) & (grp_of_lane == g_idx)
    le = jnp.where(in_grp, logits, -jnp.inf)
    m1 = jnp.max(le, axis=1, keepdims=True)
    i1 = jnp.min(jnp.where(le == m1, lane, big), axis=1, keepdims=True)
    le2 = jnp.where(lane == i1, -jnp.inf, le)
    m2 = jnp.max(le2, axis=1, keepdims=True)
    i2 = jnp.min(jnp.where(le2 == m2, lane, big), axis=1, keepdims=True)
    e2 = jnp.exp(m2 - m1)
    gate1 = g_val / (1.0 + e2)
    gate2 = g_val * e2 / (1.0 + e2)
    return jnp.where(lane == i1, gate1, 0.0) + jnp.where(lane == i2, gate2, 0.0)


def _merge_body(x_ref, oa_ref, ob_ref, wa_ref, wb_ref, wg_ref, wo_ref, g1_ref, b1_ref, wr_ref, br_ref,
                x1_ref, comb_ref, *, alpha):
    x = x_ref[...]
    d = x.shape[1]
    ya = jnp.dot(oa_ref[...], wa_ref[...], preferred_element_type=F32)
    yb = jnp.dot(ob_ref[...], wb_ref[...], preferred_element_type=F32)
    gates = jnp.dot(x.astype(BF16), wg_ref[...], preferred_element_type=F32)
    m = _sigmoid(gates[:, :d]) * ya + _sigmoid(gates[:, d:]) * yb
    mix = jnp.dot(m.astype(BF16), wo_ref[...], preferred_element_type=F32)
    x1 = _layer_norm(alpha * x + mix, g1_ref[...], b1_ref[...])
    x1_ref[...] = x1
    logits = jnp.dot(x1, wr_ref[...], preferred_element_type=F32, precision=lax.Precision.HIGHEST) + br_ref[...]
    comb_ref[...] = _route(logits)


def _merge(x, oa, ob, wa, wb, wg, wo, g1, b1, wr, br, alpha, tm):
    n, d = x.shape
    body = functools.partial(_merge_body, alpha=alpha)
    rowblk = lambda w: pl.BlockSpec((tm, w), lambda i: (i, 0))
    consts = [wa, wb, wg, wo, g1, b1, wr, br]
    return pl.pallas_call(
        body, grid=(n // tm,),
        in_specs=[rowblk(d), rowblk(oa.shape[1]), rowblk(ob.shape[1])] + [_full_spec(c) for c in consts],
        out_specs=[rowblk(d), rowblk(LANES)],
        out_shape=[jax.ShapeDtypeStruct((n, d), F32), jax.ShapeDtypeStruct((n, LANES), F32)],
        compiler_params=_cparams(("parallel",)), name="merge_ln1_route",
    )(x, oa, ob, *consts)


def _moe_body(x_ref, comb_ref, wg_ref, wu_ref, wd_ref, g2_ref, b2_ref, y_ref, xb_ref, acc_ref, *, alpha):
    e = pl.program_id(1)

    @pl.when(e == 0)
    def _():
        xb_ref[...] = x_ref[...].astype(BF16)
        acc_ref[...] = jnp.zeros_like(acc_ref)

    xb = xb_ref[...]
    hg = jnp.dot(xb, wg_ref[0], preferred_element_type=F32)
    hu = jnp.dot(xb, wu_ref[0], preferred_element_type=F32)
    comb = comb_ref[...]
    lane = lax.broadcasted_iota(I32, comb.shape, 1)
    ce = jnp.sum(jnp.where(lane == e, comb, 0.0), axis=1, keepdims=True)
    act = hg * _sigmoid(hg) * hu * ce
    acc_ref[...] += jnp.dot(act.astype(BF16), wd_ref[0], preferred_element_type=F32)

    @pl.when(e == pl.num_programs(1) - 1)
    def _():
        y_ref[...] = _layer_norm(alpha * x_ref[...] + acc_ref[...], g2_ref[...], b2_ref[...])


def _moe(x1, comb, wg, wu, wd, g2, b2, alpha, tm):
    n, d = x1.shape
    ne, _, dff = wg.shape
    body = functools.partial(_moe_body, alpha=alpha)
    return pl.pallas_call(
        body, grid=(n // tm, ne),
        in_specs=[pl.BlockSpec((tm, d), lambda i, e: (i, 0)), pl.BlockSpec((tm, LANES), lambda i, e: (i, 0)),
                  pl.BlockSpec((1, d, dff), lambda i, e: (e, 0, 0)), pl.BlockSpec((1, d, dff), lambda i, e: (e, 0, 0)),
                  pl.BlockSpec((1, dff, d), lambda i, e: (e, 0, 0)),
                  pl.BlockSpec(g2.shape, lambda i, e: (0, 0)), pl.BlockSpec(b2.shape, lambda i, e: (0, 0))],
        out_specs=pl.BlockSpec((tm, d), lambda i, e: (i, 0)),
        out_shape=jax.ShapeDtypeStruct((n, d), F32),
        scratch_shapes=[pltpu.VMEM((tm, d), BF16), pltpu.VMEM((tm, d), F32)],
        compiler_params=_cparams(("parallel", "arbitrary")), name="moe_ln2",
    )(x1, comb, wg, wu, wd, g2, b2)


def _row_tile(n, target):
    t = min(n, target)
    while n % t:
        t //= 2
    return t


def kernel(x_prompt, x_sample, cache_diff_k, cache_diff_v, cache_dsa_k, cache_dsa_v, cache_idx_k, page_table, meta_tokens, w_in, lam_q1, lam_k1, lam_q2, lam_k2, diff_norm_g, w_branch_a, w_branch_b, w_out, ln1_g, ln1_b, w_grp, b_grp, w_exp, b_exp, w_gate, w_up, w_down, ln2_g, ln2_b):
    depth = w_in.shape[0]
    assert depth == 1, "single-layer step only"
    B, S, D = x_prompt.shape
    Bd, Ts, _ = x_sample.shape
    T = S + N_META
    n_pool, page = cache_diff_k.shape[1], cache_diff_k.shape[2]
    assert page == KEY_CHUNK and S % KEY_CHUNK == 0 and Ts % 8 == 0
    n_pages = page_table.shape[1]
    past = n_pages * page
    topk_p = min(TOPK_MAX, S // 4)
    topk_s = min(TOPK_MAX, (past + Ts) // 4)
    alpha = (2.0 * depth) ** 0.25
    lam_init = 0.8 - 0.6 * math.exp(-0.3 * 0)

    w = w_in[0]
    widths = list(IN_WIDTHS[:9]) + [D, D]
    cuts = np.concatenate([[0], np.cumsum(widths)])
    w_qa, w_ka, w_va, w_qb, w_kb, w_vb, w_qi, w_ki, w_wi, w_ga, w_gb = [
        w[:, int(cuts[j]):int(cuts[j + 1])].astype(BF16) for j in range(11)]
    w_ki = jnp.pad(w_ki, ((0, 0), (0, LANES - D_I)))
    w_wi = jnp.pad(w_wi, ((0, 0), (0, LANES - H_I)))
    w_g = jnp.concatenate([w_ga, w_gb], axis=1)
    lam_vecs = [v.astype(F32) for v in (lam_q1, lam_k1, lam_q2, lam_k2)]
    gain = diff_norm_g.astype(F32)
    w_r = jnp.pad(jnp.concatenate([w_exp[0], w_grp[0]], axis=1), ((0, 0), (0, LANES - N_EXPERTS - N_GROUPS)))
    b_r = jnp.pad(jnp.concatenate([b_exp[0], b_grp[0]])[None, :], ((0, 0), (0, LANES - N_EXPERTS - N_GROUPS)))
    wa16, wb16, wo16 = w_branch_a[0].astype(BF16), w_branch_b[0].astype(BF16), w_out[0].astype(BF16)
    wg16, wu16, wd16 = w_gate[0].astype(BF16), w_up[0].astype(BF16), w_down[0].astype(BF16)

    k_segs = [(w_ka, "A", [(F32, W_A_DIM, 1.0), (BF16, W_A_DIM, 1.0)]),
              (w_va, None, [(F32, W_A_DIM, 1.0), (BF16, W_A_DIM, 1.0)]),
              (w_kb, "B", [(F32, W_B_DIM, 1.0), (BF16, W_B_DIM, 1.0)]),
              (w_vb, None, [(F32, W_B_DIM, 1.0), (BF16, W_B_DIM, 1.0)]),
              (w_ki, "A", [(F32, D_I, 1.0), (BF16, D_I, 1.0)])]
    q_segs = [(w_qa, "A", [(BF16, W_A_DIM, DH_A ** -0.5)]),
              (w_qb, "B", [(BF16, W_B_DIM, 1.0)]),
              (w_qi, "A", [(BF16, H_I * D_I, D_I ** -0.5)]),
              (w_wi, None, [(F32, H_I, H_I ** -0.5)])]

    pos_p = jnp.arange(T, dtype=jnp.int32)
    tabs_p = _rope_tables(pos_p, DH_A // 4, DH_A) + _rope_tables(pos_p, DH_B // 4, DH_B)
    tabs_q = tuple(t[N_META:] for t in tabs_p)
    meta = jnp.broadcast_to(meta_tokens[None].astype(x_prompt.dtype), (B, N_META, D))
    xp = jnp.concatenate([meta, x_prompt], axis=1)
    (ka_p, ka16, va_p, va16, kb_p, kb16, vb_p, vb16, ki_p, ki16) = _project(xp, tabs_p, k_segs, 512, "proj_k_prompt")
    qa16, qb16, qi16, wi_p = _project(x_prompt, tabs_q, q_segs, _row_tile(S, 512), "proj_q_prompt")
    tk = next(t for t in (512, 256, KEY_CHUNK) if S % t == 0)
    tq = min(256, tk)
    oa_p = _diff_prompt(qa16, ka16, va16, lam_vecs, gain, lam_init, tq, tk)
    ob_p = _dsa_prompt(qb16, qi16, wi_p, ki16, kb16, vb16, topk_p, tq, tk)

    ns = Bd * Ts
    pos_s = past + (jnp.arange(ns, dtype=jnp.int32) % Ts)
    tabs_s = _rope_tables(pos_s, DH_A // 4, DH_A) + _rope_tables(pos_s, DH_B // 4, DH_B)
    s_segs = [(wt, rope, [o for o in outs if o[0] == F32]) for (wt, rope, outs) in k_segs] + q_segs
    (ka_s, va_s, kb_s, vb_s, ki_s, qa_s, qb_s, qi_s, wi_s) = _project(
        x_sample.reshape(1, ns, D), tabs_s, s_segs, _row_tile(ns, 512), "proj_sample")
    seq = lambda a: a.reshape(Bd, Ts, a.shape[-1])
    tok_head = lambda a, h: a.reshape(Bd, Ts * h, a.shape[-1] // h)
    pool = lambda c: c.reshape(-1, c.shape[-1])
    oa_s = _diff_sample(seq(qa_s), tok_head(ka_s, H_A), tok_head(va_s, H_A), pool(cache_diff_k), pool(cache_diff_v),
                        page_table, lam_vecs, gain, lam_init)
    qi_hs = seq(qi_s).reshape(Bd, Ts, H_I, D_I).transpose(0, 2, 1, 3).reshape(Bd, H_I * Ts, D_I)
    wi_hs = seq(wi_s).transpose(0, 2, 1).reshape(Bd, H_I * Ts, 1)
    ob_s = _dsa_sample(seq(qb_s), qi_hs, wi_hs, seq(ki_s), tok_head(kb_s, H_B), tok_head(vb_s, H_B),
                       cache_idx_k.reshape(n_pool, page, D_I), pool(cache_dsa_k), pool(cache_dsa_v), page_table, topk_s)

    def tail(x2d, oa2d, ob2d):
        n = x2d.shape[0]
        x1, comb = _merge(x2d, oa2d, ob2d, wa16, wb16, w_g, wo16, ln1_g, ln1_b, w_r, b_r, alpha, _row_tile(n, 256))
        return _moe(x1, comb, wg16, wu16, wd16, ln2_g, ln2_b, alpha, _row_tile(n, 1024))

    y_prompt = tail(x_prompt.reshape(B * S, D), oa_p.reshape(B * S, -1), ob_p.reshape(B * S, -1)).reshape(B, S, D)
    y_sample = tail(x_sample.reshape(ns, D), oa_s.reshape(ns, -1), ob_s.reshape(ns, -1)).reshape(Bd, Ts, D)

    c4 = lambda a, h, dh: a.reshape(1, a.shape[0], a.shape[1], h, dh)
    return (y_prompt, y_sample,
            c4(ka_p, H_A, 2 * DH_A), c4(va_p, H_A, 2 * DH_A), c4(kb_p, H_B, DH_B), c4(vb_p, H_B, DH_B), ki_p[None],
            c4(seq(ka_s), H_A, 2 * DH_A), c4(seq(va_s), H_A, 2 * DH_A), c4(seq(kb_s), H_B, DH_B),
            c4(seq(vb_s), H_B, DH_B), seq(ki_s)[None])
```

```python
import functools
import math

import numpy as np
import jax
import jax.numpy as jnp
from jax import lax
from jax.experimental import pallas as pl
from jax.experimental.pallas import tpu as pltpu

F32 = jnp.float32
BF16 = jnp.bfloat16
I32 = jnp.int32

N_META = 16
H_A = 4
DH_A = 64
H_B = 4
DH_B = 128
H_I = 8
D_I = 64
TOPK_MAX = 256
ROPE_THETA = 500000.0
N_GROUPS = 4
EXP_PER_GROUP = 8
N_EXPERTS = N_GROUPS * EXP_PER_GROUP
LN_EPS = 1e-5
RMS_EPS = 1e-6
W_A_DIM = H_A * 2 * DH_A
W_B_DIM = H_B * DH_B
IN_WIDTHS = (W_A_DIM, W_A_DIM, W_A_DIM, W_B_DIM, W_B_DIM, W_B_DIM, H_I * D_I, D_I, H_I, None, None)

LANES = 128
KEY_CHUNK = 128
NEG = -1e30
INT_MIN = -(2 ** 31)
VMEM_LIMIT = 56 * 1024 * 1024

NT_DIMS = (((1,), (1,)), ((), ()))


def _cparams(sem):
    return pltpu.CompilerParams(dimension_semantics=sem, vmem_limit_bytes=VMEM_LIMIT)


def _full_spec(a):
    nd = a.ndim
    return pl.BlockSpec(a.shape, lambda *_: (0,) * nd)


def _rope_tables(pos, rot, period):
    half = rot // 2
    inv = jnp.power(ROPE_THETA, -jnp.arange(half, dtype=F32) / half)
    ang = pos.astype(F32)[:, None] * inv[None, :]
    cos, sin = jnp.cos(ang), jnp.sin(ang)
    n = pos.shape[0]
    zh = jnp.zeros((n, half), F32)
    rest0 = jnp.zeros((n, period - rot), F32)
    c = jnp.concatenate([cos, cos, jnp.ones((n, period - rot), F32)], axis=1)
    sn = jnp.concatenate([-sin, zh, rest0], axis=1)
    sp = jnp.concatenate([zh, sin, rest0], axis=1)
    reps = LANES // period
    return tuple(jnp.tile(t, (1, reps)) for t in (c, sn, sp))


def _rope(y, c, sn, sp, half):
    outs = []
    for j in range(y.shape[1] // LANES):
        ys = y[:, j * LANES:(j + 1) * LANES]
        outs.append(ys * c + pltpu.roll(ys, LANES - half, 1) * sn + pltpu.roll(ys, half, 1) * sp)
    return outs[0] if len(outs) == 1 else jnp.concatenate(outs, axis=1)


def _proj_body(*refs, segs):
    x_ref = refs[0]
    tabs = refs[1:7]
    n = len(segs)
    w_refs = refs[7:7 + n]
    out_refs = refs[7 + n:]
    x = x_ref[0].astype(BF16)
    oi = 0
    for (rope, outs), w_ref in zip(segs, w_refs):
        y = jnp.dot(x, w_ref[...], preferred_element_type=F32)
        if rope == "A":
            y = _rope(y, tabs[0][...], tabs[1][...], tabs[2][...], DH_A // 8)
        elif rope == "B":
            y = _rope(y, tabs[3][...], tabs[4][...], tabs[5][...], DH_B // 8)
        for (_, width, scale, heads) in outs:
            v = y[:, :width]
            if scale != 1.0:
                v = v * scale
            v = v.astype(out_refs[oi].dtype)
            if heads:
                for h in range(heads):
                    out_refs[oi][0, pl.ds(h, v.shape[0], stride=heads), :] = v[:, h * LANES:(h + 1) * LANES]
            else:
                out_refs[oi][0] = v
            oi += 1


def _project(x, tabs, segs, tm, name):
    B, R, D = x.shape
    tm = min(tm, R)
    nblk = pl.cdiv(R, tm)
    in_specs = [pl.BlockSpec((1, tm, D), lambda b, i: (b, i, 0))]
    in_specs += [pl.BlockSpec((tm, LANES), lambda b, i: (i, 0)) for _ in tabs]
    in_specs += [_full_spec(w) for (w, _, _) in segs]
    out_shape, out_specs = [], []
    for (_, _, outs) in segs:
        for (dt, width, _, heads) in outs:
            shp = (R * heads, LANES) if heads else (R, width)
            blk = (tm * heads, LANES) if heads else (tm, width)
            out_shape.append(jax.ShapeDtypeStruct((B,) + shp, dt))
            out_specs.append(pl.BlockSpec((1,) + blk, lambda b, i: (b, i, 0)))
    body = functools.partial(_proj_body, segs=tuple((rope, tuple(outs)) for (_, rope, outs) in segs))
    return pl.pallas_call(
        body, grid=(B, nblk), in_specs=in_specs, out_specs=out_specs, out_shape=out_shape,
        compiler_params=_cparams(("parallel", "parallel")), name=name,
    )(x, *tabs, *[w for (w, _, _) in segs])


def _softmax_step(s, v, m, l, acc):
    m_new = jnp.maximum(m, jnp.max(s, axis=1, keepdims=True))
    alpha = jnp.exp(m - m_new)
    p = jnp.exp(s - m_new)
    l = alpha * l + jnp.sum(p, axis=1, keepdims=True)
    acc = alpha * acc + jnp.dot(p.astype(BF16), v, preferred_element_type=F32)
    return m_new, l, acc


def _diff_lambda(lq1, lk1, lq2, lk2, lam_init):
    a = jnp.sum(lq1[...] * lk1[...], axis=1, keepdims=True)
    b = jnp.sum(lq2[...] * lk2[...], axis=1, keepdims=True)
    return jnp.exp(a) - jnp.exp(b) + lam_init


def _diff_finish(o, gain, lam_init):
    o = o * lax.rsqrt(jnp.mean(o * o, axis=1, keepdims=True) + RMS_EPS) * gain * (1.0 - lam_init)
    return o


def _group_start(g, tk):
    return pl.multiple_of(N_META + g * tk, 16)


def _diff_prompt_body(q_ref, k_ref, v_ref, lq1, lk1, lq2, lk2, g_ref, o_ref, *, lam_init, tq, tk):
    i = pl.program_id(2)
    q = q_ref[0].astype(F32)
    lane = lax.broadcasted_iota(I32, (tq, LANES), 1)
    qq = jnp.concatenate([jnp.where(lane < DH_A, q, 0.0), jnp.where(lane >= DH_A, q, 0.0)], axis=0).astype(BF16)

    def step(start, nk, mask, carry):
        kc = k_ref[0, pl.ds(start, nk), :]
        vc = v_ref[0, pl.ds(start, nk), :]
        s = lax.dot_general(qq, kc, NT_DIMS, preferred_element_type=F32)
        if mask is not None:
            s = jnp.where(mask, s, -jnp.inf)
        return _softmax_step(s, vc, *carry)

    carry = (jnp.full((2 * tq, 1), NEG, F32), jnp.zeros((2 * tq, 1), F32), jnp.zeros((2 * tq, LANES), F32))
    carry = step(0, KEY_CHUNK, lax.broadcasted_iota(I32, (2 * tq, KEY_CHUNK), 1) < N_META, carry)
    n_full = (i * tq) // tk
    carry = lax.fori_loop(0, n_full, lambda g, cr: step(_group_start(g, tk), tk, None, cr), carry)
    kx = n_full * tk + lax.broadcasted_iota(I32, (2 * tq, tk), 1)
    qx = i * tq + (lax.broadcasted_iota(I32, (2 * tq, tk), 0) & (tq - 1))
    _, l, acc = step(_group_start(n_full, tk), tk, kx <= qx, carry)

    lam = _diff_lambda(lq1, lk1, lq2, lk2, lam_init)
    o = acc[:tq] / l[:tq] - lam * (acc[tq:] / l[tq:])
    o_ref[0] = _diff_finish(o, g_ref[...], lam_init).astype(o_ref.dtype)


def _diff_prompt(qa, ka, va, lam_vecs, gain, lam_init, tq, tk):
    B, S, _ = qa.shape
    T = ka.shape[1]
    nq = S // tq
    body = functools.partial(_diff_prompt_body, lam_init=lam_init, tq=tq, tk=tk)
    res = lambda b, h, i: (b, 0, h)
    blk = lambda b, h, i: (b, i, h)
    return pl.pallas_call(
        body, grid=(B, H_A, nq),
        in_specs=[pl.BlockSpec((1, tq, LANES), blk),
                  pl.BlockSpec((1, T, LANES), res), pl.BlockSpec((1, T, LANES), res)]
                 + [_full_spec(v) for v in lam_vecs] + [_full_spec(gain)],
        out_specs=pl.BlockSpec((1, tq, LANES), blk),
        out_shape=jax.ShapeDtypeStruct((B, S, W_A_DIM), BF16),
        compiler_params=_cparams(("parallel", "parallel", "arbitrary")), name="diff_prompt",
    )(qa, ka, va, *lam_vecs, gain)


def _sort_key(x):
    b = lax.bitcast_convert_type(x, I32)
    b = jnp.where(b == INT_MIN, 0, b)
    return b ^ ((b >> 31) & 0x7FFFFFFF)


COUNT_ROWS = 128


def _count_rows(keys_ref, ngrp, cpg, pred, *operands):
    rows = keys_ref.shape[0]
    rb = min(rows, COUNT_ROWS)
    out = []
    for r0 in range(0, rows, rb):
        ops = [o[r0:r0 + rb] for o in operands]

        def one(col0, acc):
            kc = keys_ref[r0:r0 + rb, pl.ds(col0, LANES)]
            return acc + jnp.where(pred(kc, col0, *ops), 1, 0).astype(I32)

        def grp(g, acc):
            base = pl.multiple_of(LANES + g * (cpg * LANES), LANES)
            for j in range(cpg):
                acc = one(base + j * LANES, acc)
            return acc

        acc = lax.fori_loop(0, ngrp, grp, one(0, jnp.zeros((rb, LANES), I32)))
        cnt = jnp.sum(acc.astype(F32), axis=1, keepdims=True)
        out.append(jnp.broadcast_to(cnt, (rb, LANES)))
    return out[0] if len(out) == 1 else jnp.concatenate(out, axis=0)


def _select_threshold(keys_ref, ngrp, cpg, k):
    rows = keys_ref.shape[0]

    def bit_body(t, carry):
        prefix, n_ge = carry
        cand = prefix ^ lax.shift_left(jnp.int32(1), jnp.int32(31) - t)
        cnt = _count_rows(keys_ref, ngrp, cpg, lambda kc, c0, cd: kc >= cd, cand)
        take = cnt >= k
        return jnp.where(take, cand, prefix), jnp.where(take, cnt, n_ge)

    tau, n_ge = lax.fori_loop(0, 32, bit_body,
                              (jnp.full((rows, LANES), INT_MIN, I32), jnp.zeros((rows, LANES), F32)))
    tau = jnp.maximum(tau, INT_MIN + 1)
    surplus = jnp.max(jnp.maximum(n_ge - k, 0.0))

    @pl.when(surplus > 0.0)
    def _():
        n_gt = _count_rows(keys_ref, ngrp, cpg, lambda kc, c0, tu: kc > tu, tau)
        allowed = k - n_gt
        lane = lax.broadcasted_iota(I32, (rows, LANES), 1)
        nbits = max(1, int(math.ceil(math.log2(keys_ref.shape[1] + 1))))

        def col_body(t, lim):
            cand = lim | lax.shift_left(jnp.int32(1), jnp.int32(nbits - 1) - t)
            cnt = _count_rows(
                keys_ref, ngrp, cpg,
                lambda kc, c0, tu, cd: (kc == tu) & (lax.broadcasted_iota(I32, kc.shape, 1) + c0 < cd), tau, cand)
            return jnp.where(cnt < allowed, cand, lim)

        lim = lax.fori_loop(0, nbits, col_body, jnp.zeros((rows, LANES), I32))

        def drop(c, carry):
            c0 = pl.multiple_of(c * LANES, LANES)
            kc = keys_ref[:, pl.ds(c0, LANES)]
            keys_ref[:, pl.ds(c0, LANES)] = jnp.where((kc == tau) & (lane + c0 > lim), INT_MIN, kc)
            return carry

        lax.fori_loop(0, 1 + ngrp * cpg, drop, 0)

    return tau


SCORE_BLOCK = 256


def _masked_softmax_step(s, sel, v, c2, m, l, acc):
    s = jnp.where(sel, s, -jnp.inf)
    m_new = jnp.maximum(m, jnp.max(s, axis=1, keepdims=True))
    alpha = jnp.exp2((m - m_new) * c2)
    p = jnp.exp2((s - m_new) * c2)
    l = alpha * l + jnp.sum(p, axis=1, keepdims=True)
    acc = alpha * acc + jnp.dot(p.astype(BF16), v, preferred_element_type=F32)
    return m_new, l, acc


def _dsa_prompt_body(qb_ref, qi_ref, wi_ref, ki_ref, kb_ref, vb_ref, o_ref, keys_ref, *, tq, tk, topk):
    i = pl.program_id(1)
    n_full = (i * tq) // tk
    ngrp = n_full + 1
    cpg = tk // LANES
    lane = lax.broadcasted_iota(I32, (tq, LANES), 1)
    row = lax.broadcasted_iota(I32, (tq, LANES), 0)

    qi = qi_ref[0]
    wi = wi_ref[0]
    qst = jnp.concatenate([qi[:, h * D_I:(h + 1) * D_I] for h in range(H_I)], axis=0)
    wh = [jnp.broadcast_to(wi[:, h:h + 1], (tq, LANES)) for h in range(H_I)]

    def score(start, nk, col0, x0):
        kc = ki_ref[0, pl.ds(start, nk), :]
        r = lax.dot_general(qst, kc, NT_DIMS, preferred_element_type=F32)
        for j in range(nk // LANES):
            sc = jnp.zeros((tq, LANES), F32)
            for h in range(H_I):
                sc = sc + wh[h] * jnp.maximum(r[h * tq:(h + 1) * tq, j * LANES:(j + 1) * LANES], 0.0)
            key = _sort_key(sc)
            if x0 is None:
                key = jnp.where(lane < N_META, key, INT_MIN)
            elif x0 is not False:
                key = jnp.where(x0 + j * LANES + lane <= i * tq + row, key, INT_MIN)
            keys_ref[:, pl.ds(col0 + j * LANES, LANES)] = key

    score(0, KEY_CHUNK, 0, None)

    def score_group(g, masked):
        sb = min(SCORE_BLOCK, tk)
        for jb in range(tk // sb):
            off = g * tk + jb * sb
            score(pl.multiple_of(N_META + off, 16), sb, pl.multiple_of(LANES + off, LANES),
                  off if masked else False)

    def score_loop(g, carry):
        score_group(g, False)
        return carry

    lax.fori_loop(0, n_full, score_loop, 0)
    score_group(n_full, True)

    tau = _select_threshold(keys_ref, ngrp, cpg, topk)

    q = qb_ref[0]
    qs = [q[:, h * DH_B:(h + 1) * DH_B] for h in range(H_B)]
    c2 = (DH_B ** -0.5) * math.log2(math.e)

    def att(start, nk, col0, carry):
        taub = tau if nk == LANES else jnp.concatenate([tau] * (nk // LANES), axis=1)
        sel = keys_ref[:, pl.ds(col0, nk)] >= taub
        kc = kb_ref[0, pl.ds(start, nk), :]
        vc = vb_ref[0, pl.ds(start, nk), :]
        out = []
        for h in range(H_B):
            hs = slice(h * DH_B, (h + 1) * DH_B)
            s = lax.dot_general(qs[h], kc[:, hs], NT_DIMS, preferred_element_type=F32)
            out.append(_masked_softmax_step(s, sel, vc[:, hs], c2, *carry[h]))
        return tuple(out)

    init = tuple((jnp.full((tq, 1), NEG, F32), jnp.zeros((tq, 1), F32), jnp.zeros((tq, DH_B), F32))
                 for _ in range(H_B))
    carry = att(0, KEY_CHUNK, 0, init)
    carry = lax.fori_loop(
        0, ngrp, lambda g, cr: att(_group_start(g, tk), tk, pl.multiple_of(LANES + g * tk, LANES), cr), carry)
    for h in range(H_B):
        _, l, acc = carry[h]
        o_ref[0, :, h * DH_B:(h + 1) * DH_B] = (acc / l).astype(o_ref.dtype)


def _dsa_prompt(qb, qi, wi, ki, kb, vb, topk, tq, tk):
    B, S, _ = qb.shape
    T = kb.shape[1]
    nq = S // tq
    body = functools.partial(_dsa_prompt_body, tq=tq, tk=tk, topk=topk)
    blk = lambda b, i: (b, i, 0)
    res = lambda b, i: (b, 0, 0)
    once = pl.Buffered(1)
    return pl.pallas_call(
        body, grid=(B, nq),
        in_specs=[pl.BlockSpec((1, tq, W_B_DIM), blk), pl.BlockSpec((1, tq, H_I * D_I), blk),
                  pl.BlockSpec((1, tq, H_I), blk),
                  pl.BlockSpec((1, T, D_I), res, pipeline_mode=once),
                  pl.BlockSpec((1, T, W_B_DIM), res, pipeline_mode=once),
                  pl.BlockSpec((1, T, W_B_DIM), res, pipeline_mode=once)],
        out_specs=pl.BlockSpec((1, tq, W_B_DIM), blk),
        out_shape=jax.ShapeDtypeStruct((B, S, W_B_DIM), BF16),
        scratch_shapes=[pltpu.VMEM((tq, LANES + S), I32)],
        compiler_params=_cparams(("parallel", "arbitrary")), name="dsa_prompt",
    )(qb, qi, wi, ki, kb, vb)


def _div(x, d):
    assert d & (d - 1) == 0
    return x >> (d.bit_length() - 1)


def _mod(x, d):
    assert d & (d - 1) == 0
    return x & (d - 1)


def _pad_rows(a, rows):
    return jnp.concatenate([a, jnp.zeros((rows - a.shape[0], a.shape[1]), a.dtype)], axis=0)


def _page_specs(n_pages, rows, width):
    return [pl.BlockSpec((rows, width), functools.partial(lambda b, pt, p: (pt[b, p], 0), p=p))
            for p in range(n_pages)]


def _head_rows(q, heads):
    return jnp.concatenate([q[:, h * LANES:(h + 1) * LANES] for h in range(heads)], axis=0)


def _diff_sample_body(pt_ref, q_ref, kn_ref, vn_ref, *rest, n_pages, lam_init, ts):
    k_pages = rest[:n_pages]
    v_pages = rest[n_pages:2 * n_pages]
    lq1, lk1, lq2, lk2, g_ref, o_ref = rest[2 * n_pages:]
    lam = _diff_lambda(lq1, lk1, lq2, lk2, lam_init)
    rows = H_A * ts
    pw = KEY_CHUNK * H_A
    qh = _head_rows(q_ref[0].astype(F32), H_A)
    lane = lax.broadcasted_iota(I32, (rows, LANES), 1)
    qq = jnp.concatenate([jnp.where(lane < DH_A, qh, 0.0), jnp.where(lane >= DH_A, qh, 0.0)], axis=0).astype(BF16)
    kn = _pad_rows(kn_ref[0], LANES).astype(BF16)
    vn = _pad_rows(vn_ref[0], LANES).astype(BF16)
    r2 = lax.broadcasted_iota(I32, (2 * rows, pw), 0)
    c2 = lax.broadcasted_iota(I32, (2 * rows, pw), 1)
    same_head = _mod(_div(r2, ts), H_A) == _mod(c2, H_A)
    rn = lax.broadcasted_iota(I32, (2 * rows, LANES), 0)
    cn = lax.broadcasted_iota(I32, (2 * rows, LANES), 1)
    new_ok = (_mod(_div(rn, ts), H_A) == _mod(cn, H_A)) & (_div(cn, H_A) <= _mod(rn, ts)) & (cn < ts * H_A)
    s = jnp.concatenate(
        [jnp.where(same_head, lax.dot_general(qq, k_pages[p][...].astype(BF16), NT_DIMS,
                                              preferred_element_type=F32), -jnp.inf) for p in range(n_pages)]
        + [jnp.where(new_ok, lax.dot_general(qq, kn, NT_DIMS, preferred_element_type=F32), -jnp.inf)], axis=1)
    m = jnp.max(s, axis=1, keepdims=True)
    pr = jnp.exp(s - m)
    pr = pr / jnp.sum(pr, axis=1, keepdims=True)
    wgt = (pr[:rows] - lam * pr[rows:]).astype(BF16)
    o = jnp.dot(wgt[:, n_pages * pw:], vn, preferred_element_type=F32)
    for p in range(n_pages):
        o = o + jnp.dot(wgt[:, p * pw:(p + 1) * pw], v_pages[p][...].astype(BF16), preferred_element_type=F32)
    o = _diff_finish(o, g_ref[...], lam_init)
    for h in range(H_A):
        o_ref[0, :, h * LANES:(h + 1) * LANES] = o[h * ts:(h + 1) * ts].astype(o_ref.dtype)


def _diff_sample(qa, ka_new, va_new, pool_k, pool_v, page_table, lam_vecs, gain, lam_init):
    Bd, ts, _ = qa.shape
    n_pages = page_table.shape[1]
    pw = KEY_CHUNK * H_A
    body = functools.partial(_diff_sample_body, n_pages=n_pages, lam_init=lam_init, ts=ts)
    blk = lambda b, pt: (b, 0, 0)
    const2 = lambda b, pt: (0, 0)
    new_spec = pl.BlockSpec((1, ts * H_A, LANES), blk)
    grid_spec = pltpu.PrefetchScalarGridSpec(
        num_scalar_prefetch=1, grid=(Bd,),
        in_specs=[pl.BlockSpec((1, ts, W_A_DIM), blk), new_spec, new_spec]
                 + _page_specs(n_pages, pw, LANES) + _page_specs(n_pages, pw, LANES)
                 + [pl.BlockSpec(v.shape, const2) for v in lam_vecs] + [pl.BlockSpec(gain.shape, const2)],
        out_specs=pl.BlockSpec((1, ts, W_A_DIM), blk))
    return pl.pallas_call(
        body, grid_spec=grid_spec, out_shape=jax.ShapeDtypeStruct((Bd, ts, W_A_DIM), BF16),
        compiler_params=_cparams(("arbitrary",)), name="diff_sample",
    )(page_table, qa, ka_new, va_new, *([pool_k] * n_pages), *([pool_v] * n_pages), *lam_vecs, gain)


def _dsa_sample_body(pt_ref, qb_ref, qi_ref, wi_ref, kin_ref, kn_ref, vn_ref, *rest, n_pages, ts, topk, cpg):
    ki_pages = rest[:n_pages]
    k_pages = rest[n_pages:2 * n_pages]
    v_pages = rest[2 * n_pages:3 * n_pages]
    o_ref, keys_ref = rest[3 * n_pages:]
    rows_i = H_I * ts
    qi = qi_ref[0]
    wcol = jnp.broadcast_to(wi_ref[0], (rows_i, LANES))
    lane = lax.broadcasted_iota(I32, (ts, LANES), 1)
    row = lax.broadcasted_iota(I32, (ts, LANES), 0)

    def idx_scores(kc):
        r = lax.dot_general(qi, kc, NT_DIMS, preferred_element_type=F32)
        r = wcol * jnp.maximum(r, 0.0)
        sc = r[0:ts]
        for h in range(1, H_I):
            sc = sc + r[h * ts:(h + 1) * ts]
        return _sort_key(sc)

    for p in range(n_pages):
        keys_ref[:, p * LANES:(p + 1) * LANES] = idx_scores(ki_pages[p][0].astype(BF16))
    kin = _pad_rows(kin_ref[0], KEY_CHUNK).astype(BF16)
    keys_ref[:, n_pages * LANES:] = jnp.where((lane <= row) & (lane < ts), idx_scores(kin), INT_MIN)

    tau = _select_threshold(keys_ref, n_pages // cpg, cpg, topk)

    pw = KEY_CHUNK * H_B
    rows = H_B * ts
    expand = jnp.where(_div(lax.broadcasted_iota(I32, (KEY_CHUNK, pw), 1), H_B)
                       == lax.broadcasted_iota(I32, (KEY_CHUNK, pw), 0), 1.0, 0.0).astype(BF16)
    r2 = lax.broadcasted_iota(I32, (rows, pw), 0)
    c2i = lax.broadcasted_iota(I32, (rows, pw), 1)
    same_head = _div(r2, ts) == _mod(c2i, H_B)

    def selected(p):
        sel = jnp.where(keys_ref[:, p * LANES:(p + 1) * LANES] >= tau, 1.0, 0.0)
        sel = jnp.dot(_pad_rows(sel, 2 * ts).astype(BF16), expand, preferred_element_type=F32)[:ts]
        return same_head & (jnp.concatenate([sel] * H_B, axis=0) > 0.5)

    qh = _head_rows(qb_ref[0].astype(F32), H_B).astype(BF16)
    kn = _pad_rows(kn_ref[0], LANES).astype(BF16)
    vn = _pad_rows(vn_ref[0], LANES).astype(BF16)
    c2 = (DH_B ** -0.5) * math.log2(math.e)
    s = jnp.concatenate(
        [jnp.where(selected(p), lax.dot_general(qh, k_pages[p][...].astype(BF16), NT_DIMS,
                                                preferred_element_type=F32), -jnp.inf) for p in range(n_pages)]
        + [jnp.where(selected(n_pages)[:, :LANES], lax.dot_general(qh, kn, NT_DIMS, preferred_element_type=F32),
                     -jnp.inf)], axis=1)
    m = jnp.max(s, axis=1, keepdims=True)
    pr = jnp.exp2((s - m) * c2)
    pr = (pr / jnp.sum(pr, axis=1, keepdims=True)).astype(BF16)
    o = jnp.dot(pr[:, n_pages * pw:], vn, preferred_element_type=F32)
    for p in range(n_pages):
        o = o + jnp.dot(pr[:, p * pw:(p + 1) * pw], v_pages[p][...].astype(BF16), preferred_element_type=F32)
    for h in range(H_B):
        o_ref[0, :, h * LANES:(h + 1) * LANES] = o[h * ts:(h + 1) * ts].astype(o_ref.dtype)


def _dsa_sample(qb, qi_hs, wi_hs, ki_new, kb_new, vb_new, pool_ki, pool_k, pool_v, page_table, topk):
    Bd, ts, _ = qb.shape
    n_pages = page_table.shape[1]
    pw = KEY_CHUNK * H_B
    cpg = 4 if n_pages % 4 == 0 else 1
    body = functools.partial(_dsa_sample_body, n_pages=n_pages, ts=ts, topk=topk, cpg=cpg)
    blk = lambda b, pt: (b, 0, 0)
    new_spec = pl.BlockSpec((1, ts * H_B, LANES), blk)
    ki_specs = [pl.BlockSpec((1, KEY_CHUNK, D_I), functools.partial(lambda b, pt, p: (pt[b, p], 0, 0), p=p))
                for p in range(n_pages)]
    grid_spec = pltpu.PrefetchScalarGridSpec(
        num_scalar_prefetch=1, grid=(Bd,),
        in_specs=[pl.BlockSpec((1, ts, W_B_DIM), blk), pl.BlockSpec((1, H_I * ts, D_I), blk),
                  pl.BlockSpec((1, H_I * ts, 1), blk), pl.BlockSpec((1, ts, D_I), blk), new_spec, new_spec]
                 + ki_specs + _page_specs(n_pages, pw, LANES) + _page_specs(n_pages, pw, LANES),
        out_specs=pl.BlockSpec((1, ts, W_B_DIM), blk),
        scratch_shapes=[pltpu.VMEM((ts, (n_pages + 1) * LANES), I32)])
    return pl.pallas_call(
        body, grid_spec=grid_spec, out_shape=jax.ShapeDtypeStruct((Bd, ts, W_B_DIM), BF16),
        compiler_params=_cparams(("arbitrary",)), name="dsa_sample",
    )(page_table, qb, qi_hs, wi_hs, ki_new, kb_new, vb_new,
      *([pool_ki] * n_pages), *([pool_k] * n_pages), *([pool_v] * n_pages))


def _layer_norm(h, g, b):
    mu = jnp.mean(h, axis=1, keepdims=True)
    d = h - mu
    var = jnp.mean(d * d, axis=1, keepdims=True)
    return d * lax.rsqrt(var + LN_EPS) * g + b


def _sigmoid(x):
    return 1.0 / (1.0 + jnp.exp(-x))


def _route(logits):
    tm = logits.shape[0]
    lane_i = lax.broadcasted_iota(I32, (tm, LANES), 1)
    lane = lane_i.astype(F32)
    big = float(LANES)
    is_grp = (lane_i >= N_EXPERTS) & (lane_i < N_EXPERTS + N_GROUPS)
    lg = jnp.where(is_grp, logits, -jnp.inf)
    gmax = jnp.max(lg, axis=1, keepdims=True)
    g_val = 1.0 / jnp.sum(jnp.exp(lg - gmax), axis=1, keepdims=True)
    g_idx = jnp.min(jnp.where(lg == gmax, lane - N_EXPERTS, big), axis=1, keepdims=True)
    grp_of_lane = (lane_i >> int(math.log2(EXP_PER_GROUP))).astype(F32)
    in_grp = (lane_i < N_EXPERTS) & (grp_of_lane == g_idx)
    le = jnp.where(in_grp, logits, -jnp.inf)
    m1 = jnp.max(le, axis=1, keepdims=True)
    i1 = jnp.min(jnp.where(le == m1, lane, big), axis=1, keepdims=True)
    le2 = jnp.where(lane == i1, -jnp.inf, le)
    m2 = jnp.max(le2, axis=1, keepdims=True)
    i2 = jnp.min(jnp.where(le2 == m2, lane, big), axis=1, keepdims=True)
    e2 = jnp.exp(m2 - m1)
    gate1 = g_val / (1.0 + e2)
    gate2 = g_val * e2 / (1.0 + e2)
    return jnp.where(lane == i1, gate1, 0.0) + jnp.where(lane == i2, gate2, 0.0)


def _merge_body(x_ref, oa_ref, ob_ref, wa_ref, wb_ref, wg_ref, wo_ref, g1_ref, b1_ref, wr_ref, br_ref,
                x1_ref, comb_ref, *, alpha):
    x = x_ref[...]
    d = x.shape[1]
    ya = jnp.dot(oa_ref[...], wa_ref[...], preferred_element_type=F32)
    yb = jnp.dot(ob_ref[...], wb_ref[...], preferred_element_type=F32)
    gates = jnp.dot(x.astype(BF16), wg_ref[...], preferred_element_type=F32)
    m = _sigmoid(gates[:, :d]) * ya + _sigmoid(gates[:, d:]) * yb
    mix = jnp.dot(m.astype(BF16), wo_ref[...], preferred_element_type=F32)
    x1 = _layer_norm(alpha * x + mix, g1_ref[...], b1_ref[...])
    x1_ref[...] = x1
    logits = jnp.dot(x1, wr_ref[...], preferred_element_type=F32, precision=lax.Precision.HIGHEST) + br_ref[...]
    comb_ref[...] = _route(logits)


def _merge(x, oa, ob, wa, wb, wg, wo, g1, b1, wr, br, alpha, tm):
    n, d = x.shape
    body = functools.partial(_merge_body, alpha=alpha)
    rowblk = lambda w: pl.BlockSpec((tm, w), lambda i: (i, 0))
    consts = [wa, wb, wg, wo, g1, b1, wr, br]
    return pl.pallas_call(
        body, grid=(n // tm,),
        in_specs=[rowblk(d), rowblk(oa.shape[1]), rowblk(ob.shape[1])] + [_full_spec(c) for c in consts],
        out_specs=[rowblk(d), rowblk(LANES)],
        out_shape=[jax.ShapeDtypeStruct((n, d), F32), jax.ShapeDtypeStruct((n, LANES), F32)],
        compiler_params=_cparams(("parallel",)), name="merge_ln1_route",
    )(x, oa, ob, *consts)


def _moe_body(x_ref, comb_ref, wg_ref, wu_ref, wd_ref, g2_ref, b2_ref, y_ref, xb_ref, acc_ref, *, alpha):
    e = pl.program_id(1)

    @pl.when(e == 0)
    def _():
        xb_ref[...] = x_ref[...].astype(BF16)
        acc_ref[...] = jnp.zeros_like(acc_ref)

    xb = xb_ref[...]
    hg = jnp.dot(xb, wg_ref[0], preferred_element_type=F32)
    hu = jnp.dot(xb, wu_ref[0], preferred_element_type=F32)
    comb = comb_ref[...]
    lane = lax.broadcasted_iota(I32, comb.shape, 1)
    ce = jnp.sum(jnp.where(lane == e, comb, 0.0), axis=1, keepdims=True)
    act = hg * _sigmoid(hg) * hu * ce
    acc_ref[...] += jnp.dot(act.astype(BF16), wd_ref[0], preferred_element_type=F32)

    @pl.when(e == pl.num_programs(1) - 1)
    def _():
        y_ref[...] = _layer_norm(alpha * x_ref[...] + acc_ref[...], g2_ref[...], b2_ref[...])


def _moe(x1, comb, wg, wu, wd, g2, b2, alpha, tm):
    n, d = x1.shape
    ne, _, dff = wg.shape
    body = functools.partial(_moe_body, alpha=alpha)
    return pl.pallas_call(
        body, grid=(n // tm, ne),
        in_specs=[pl.BlockSpec((tm, d), lambda i, e: (i, 0)), pl.BlockSpec((tm, LANES), lambda i, e: (i, 0)),
                  pl.BlockSpec((1, d, dff), lambda i, e: (e, 0, 0)), pl.BlockSpec((1, d, dff), lambda i, e: (e, 0, 0)),
                  pl.BlockSpec((1, dff, d), lambda i, e: (e, 0, 0)),
                  pl.BlockSpec(g2.shape, lambda i, e: (0, 0)), pl.BlockSpec(b2.shape, lambda i, e: (0, 0))],
        out_specs=pl.BlockSpec((tm, d), lambda i, e: (i, 0)),
        out_shape=jax.ShapeDtypeStruct((n, d), F32),
        scratch_shapes=[pltpu.VMEM((tm, d), BF16), pltpu.VMEM((tm, d), F32)],
        compiler_params=_cparams(("parallel", "arbitrary")), name="moe_ln2",
    )(x1, comb, wg, wu, wd, g2, b2)


def _row_tile(n, target):
    t = min(n, target)
    while n % t:
        t //= 2
    return t


def kernel(x_prompt, x_sample, cache_diff_k, cache_diff_v, cache_dsa_k, cache_dsa_v, cache_idx_k, page_table, meta_tokens, w_in, lam_q1, lam_k1, lam_q2, lam_k2, diff_norm_g, w_branch_a, w_branch_b, w_out, ln1_g, ln1_b, w_grp, b_grp, w_exp, b_exp, w_gate, w_up, w_down, ln2_g, ln2_b):
    depth = w_in.shape[0]
    assert depth == 1, "single-layer step only"
    B, S, D = x_prompt.shape
    Bd, Ts, _ = x_sample.shape
    T = S + N_META
    n_pool, page = cache_diff_k.shape[1], cache_diff_k.shape[2]
    assert page == KEY_CHUNK and S % KEY_CHUNK == 0 and Ts % 8 == 0
    n_pages = page_table.shape[1]
    past = n_pages * page
    topk_p = min(TOPK_MAX, S // 4)
    topk_s = min(TOPK_MAX, (past + Ts) // 4)
    alpha = (2.0 * depth) ** 0.25
    lam_init = 0.8 - 0.6 * math.exp(-0.3 * 0)

    w = w_in[0]
    widths = list(IN_WIDTHS[:9]) + [D, D]
    cuts = np.concatenate([[0], np.cumsum(widths)])
    w_qa, w_ka, w_va, w_qb, w_kb, w_vb, w_qi, w_ki, w_wi, w_ga, w_gb = [
        w[:, int(cuts[j]):int(cuts[j + 1])].astype(BF16) for j in range(11)]
    w_ki = jnp.pad(w_ki, ((0, 0), (0, LANES - D_I)))
    w_wi = jnp.pad(w_wi, ((0, 0), (0, LANES - H_I)))
    w_g = jnp.concatenate([w_ga, w_gb], axis=1)
    lam_vecs = [v.astype(F32) for v in (lam_q1, lam_k1, lam_q2, lam_k2)]
    gain = diff_norm_g.astype(F32)
    w_r = jnp.pad(jnp.concatenate([w_exp[0], w_grp[0]], axis=1), ((0, 0), (0, LANES - N_EXPERTS - N_GROUPS)))
    b_r = jnp.pad(jnp.concatenate([b_exp[0], b_grp[0]])[None, :], ((0, 0), (0, LANES - N_EXPERTS - N_GROUPS)))
    wa16, wb16, wo16 = w_branch_a[0].astype(BF16), w_branch_b[0].astype(BF16), w_out[0].astype(BF16)
    wg16, wu16, wd16 = w_gate[0].astype(BF16), w_up[0].astype(BF16), w_down[0].astype(BF16)

    k_segs = [(w_ka, "A", [(F32, W_A_DIM, 1.0, H_A), (BF16, W_A_DIM, 1.0, 0)]),
              (w_va, None, [(F32, W_A_DIM, 1.0, H_A), (BF16, W_A_DIM, 1.0, 0)]),
              (w_kb, "B", [(F32, W_B_DIM, 1.0, H_B), (BF16, W_B_DIM, 1.0, 0)]),
              (w_vb, None, [(F32, W_B_DIM, 1.0, H_B), (BF16, W_B_DIM, 1.0, 0)]),
              (w_ki, "A", [(F32, D_I, 1.0, 0), (BF16, D_I, 1.0, 0)])]
    q_segs = [(w_qa, "A", [(BF16, W_A_DIM, DH_A ** -0.5, 0)]),
              (w_qb, "B", [(BF16, W_B_DIM, 1.0, 0)]),
              (w_qi, "A", [(BF16, H_I * D_I, D_I ** -0.5, 0)]),
              (w_wi, None, [(F32, H_I, H_I ** -0.5, 0)])]

    pos_p = jnp.arange(T, dtype=jnp.int32)
    tabs_p = _rope_tables(pos_p, DH_A // 4, DH_A) + _rope_tables(pos_p, DH_B // 4, DH_B)
    tabs_q = tuple(t[N_META:] for t in tabs_p)
    meta = jnp.broadcast_to(meta_tokens[None].astype(x_prompt.dtype), (B, N_META, D))
    xp = jnp.concatenate([meta, x_prompt], axis=1)
    (ka_p, ka16, va_p, va16, kb_p, kb16, vb_p, vb16, ki_p, ki16) = _project(xp, tabs_p, k_segs, 512, "proj_k_prompt")
    qa16, qb16, qi16, wi_p = _project(x_prompt, tabs_q, q_segs, _row_tile(S, 512), "proj_q_prompt")
    tk = next(t for t in (512, 256, KEY_CHUNK) if S % t == 0)
    tq = min(256, tk)
    oa_p = _diff_prompt(qa16, ka16, va16, lam_vecs, gain, lam_init, tq, tk)
    ob_p = _dsa_prompt(qb16, qi16, wi_p, ki16, kb16, vb16, topk_p, tq, tk)

    ns = Bd * Ts
    pos_s = past + (jnp.arange(ns, dtype=jnp.int32) % Ts)
    tabs_s = _rope_tables(pos_s, DH_A // 4, DH_A) + _rope_tables(pos_s, DH_B // 4, DH_B)
    s_segs = [(wt, rope, [o for o in outs if o[0] == F32]) for (wt, rope, outs) in k_segs] + q_segs
    (ka_s, va_s, kb_s, vb_s, ki_s, qa_s, qb_s, qi_s, wi_s) = _project(
        x_sample.reshape(1, ns, D), tabs_s, s_segs, _row_tile(ns, 512), "proj_sample")
    seq = lambda a: a.reshape(Bd, -1, a.shape[-1])
    pool = lambda c: c.reshape(-1, c.shape[-1])
    oa_s = _diff_sample(seq(qa_s), seq(ka_s), seq(va_s), pool(cache_diff_k), pool(cache_diff_v),
                        page_table, lam_vecs, gain, lam_init)
    qi_hs = seq(qi_s).reshape(Bd, Ts, H_I, D_I).transpose(0, 2, 1, 3).reshape(Bd, H_I * Ts, D_I)
    wi_hs = seq(wi_s).transpose(0, 2, 1).reshape(Bd, H_I * Ts, 1)
    ob_s = _dsa_sample(seq(qb_s), qi_hs, wi_hs, seq(ki_s), seq(kb_s), seq(vb_s),
                       cache_idx_k.reshape(n_pool, page, D_I), pool(cache_dsa_k), pool(cache_dsa_v), page_table, topk_s)

    def tail(x2d, oa2d, ob2d):
        n = x2d.shape[0]
        x1, comb = _merge(x2d, oa2d, ob2d, wa16, wb16, w_g, wo16, ln1_g, ln1_b, w_r, b_r, alpha, _row_tile(n, 512))
        return _moe(x1, comb, wg16, wu16, wd16, ln2_g, ln2_b, alpha, _row_tile(n, 1024))

    y_prompt = tail(x_prompt.reshape(B * S, D), oa_p.reshape(B * S, -1), ob_p.reshape(B * S, -1)).reshape(B, S, D)
    y_sample = tail(x_sample.reshape(ns, D), oa_s.reshape(ns, -1), ob_s.reshape(ns, -1)).reshape(Bd, Ts, D)

    cache_p = lambda a, h: a.reshape(1, B, T, h, LANES)
    cache_s = lambda a, h: a.reshape(1, Bd, Ts, h, LANES)
    return (y_prompt, y_sample,
            cache_p(ka_p, H_A), cache_p(va_p, H_A), cache_p(kb_p, H_B), cache_p(vb_p, H_B), ki_p[None],
            cache_s(ka_s, H_A), cache_s(va_s, H_A), cache_s(kb_s, H_B), cache_s(vb_s, H_B), seq(ki_s)[None])
```

```python
import functools
import math

import numpy as np
import jax
import jax.numpy as jnp
from jax import lax
from jax.experimental import pallas as pl
from jax.experimental.pallas import tpu as pltpu

F32 = jnp.float32
BF16 = jnp.bfloat16
I32 = jnp.int32

N_META = 16
H_A = 4
DH_A = 64
H_B = 4
DH_B = 128
H_I = 8
D_I = 64
TOPK_MAX = 256
ROPE_THETA = 500000.0
N_GROUPS = 4
EXP_PER_GROUP = 8
N_EXPERTS = N_GROUPS * EXP_PER_GROUP
LN_EPS = 1e-5
RMS_EPS = 1e-6
W_A_DIM = H_A * 2 * DH_A
W_B_DIM = H_B * DH_B
IN_WIDTHS = (W_A_DIM, W_A_DIM, W_A_DIM, W_B_DIM, W_B_DIM, W_B_DIM, H_I * D_I, D_I, H_I, None, None)

LANES = 128
KEY_CHUNK = 128
NEG = -1e30
INT_MIN = -(2 ** 31)
VMEM_LIMIT = 56 * 1024 * 1024

NT_DIMS = (((1,), (1,)), ((), ()))


def _cparams(sem):
    return pltpu.CompilerParams(dimension_semantics=sem, vmem_limit_bytes=VMEM_LIMIT)


def _full_spec(a):
    nd = a.ndim
    return pl.BlockSpec(a.shape, lambda *_: (0,) * nd)


def _rope_tables(pos, rot, period):
    half = rot // 2
    inv = jnp.power(ROPE_THETA, -jnp.arange(half, dtype=F32) / half)
    ang = pos.astype(F32)[:, None] * inv[None, :]
    cos, sin = jnp.cos(ang), jnp.sin(ang)
    n = pos.shape[0]
    zh = jnp.zeros((n, half), F32)
    rest0 = jnp.zeros((n, period - rot), F32)
    c = jnp.concatenate([cos, cos, jnp.ones((n, period - rot), F32)], axis=1)
    sn = jnp.concatenate([-sin, zh, rest0], axis=1)
    sp = jnp.concatenate([zh, sin, rest0], axis=1)
    reps = LANES // period
    return tuple(jnp.tile(t, (1, reps)) for t in (c, sn, sp))


def _rope(y, c, sn, sp, half):
    outs = []
    for j in range(y.shape[1] // LANES):
        ys = y[:, j * LANES:(j + 1) * LANES]
        outs.append(ys * c + pltpu.roll(ys, LANES - half, 1) * sn + pltpu.roll(ys, half, 1) * sp)
    return outs[0] if len(outs) == 1 else jnp.concatenate(outs, axis=1)


def _proj_body(*refs, segs):
    x_ref = refs[0]
    tabs = refs[1:7]
    n = len(segs)
    w_refs = refs[7:7 + n]
    out_refs = refs[7 + n:]
    x = x_ref[0].astype(BF16)
    oi = 0
    for (rope, outs), w_ref in zip(segs, w_refs):
        y = jnp.dot(x, w_ref[...], preferred_element_type=F32)
        if rope == "A":
            y = _rope(y, tabs[0][...], tabs[1][...], tabs[2][...], DH_A // 8)
        elif rope == "B":
            y = _rope(y, tabs[3][...], tabs[4][...], tabs[5][...], DH_B // 8)
        for (_, width, scale, heads) in outs:
            v = y[:, :width]
            if scale != 1.0:
                v = v * scale
            v = v.astype(out_refs[oi].dtype)
            if heads:
                for h in range(heads):
                    out_refs[oi][0, pl.ds(h, v.shape[0], stride=heads), :] = v[:, h * LANES:(h + 1) * LANES]
            else:
                out_refs[oi][0] = v
            oi += 1


def _project(x, tabs, segs, tm, name):
    B, R, D = x.shape
    tm = min(tm, R)
    nblk = pl.cdiv(R, tm)
    in_specs = [pl.BlockSpec((1, tm, D), lambda b, i: (b, i, 0))]
    in_specs += [pl.BlockSpec((tm, LANES), lambda b, i: (i, 0)) for _ in tabs]
    in_specs += [_full_spec(w) for (w, _, _) in segs]
    out_shape, out_specs = [], []
    for (_, _, outs) in segs:
        for (dt, width, _, heads) in outs:
            shp = (R * heads, LANES) if heads else (R, width)
            blk = (tm * heads, LANES) if heads else (tm, width)
            out_shape.append(jax.ShapeDtypeStruct((B,) + shp, dt))
            out_specs.append(pl.BlockSpec((1,) + blk, lambda b, i: (b, i, 0)))
    body = functools.partial(_proj_body, segs=tuple((rope, tuple(outs)) for (_, rope, outs) in segs))
    return pl.pallas_call(
        body, grid=(B, nblk), in_specs=in_specs, out_specs=out_specs, out_shape=out_shape,
        compiler_params=_cparams(("parallel", "parallel")), name=name,
    )(x, *tabs, *[w for (w, _, _) in segs])


def _softmax_step(s, v, m, l, acc):
    m_new = jnp.maximum(m, jnp.max(s, axis=1, keepdims=True))
    alpha = jnp.exp(m - m_new)
    p = jnp.exp(s - m_new)
    l = alpha * l + jnp.sum(p, axis=1, keepdims=True)
    acc = alpha * acc + jnp.dot(p.astype(BF16), v, preferred_element_type=F32)
    return m_new, l, acc


def _diff_lambda(lq1, lk1, lq2, lk2, lam_init):
    a = jnp.sum(lq1[...] * lk1[...], axis=1, keepdims=True)
    b = jnp.sum(lq2[...] * lk2[...], axis=1, keepdims=True)
    return jnp.exp(a) - jnp.exp(b) + lam_init


def _diff_finish(o, gain, lam_init):
    o = o * lax.rsqrt(jnp.mean(o * o, axis=1, keepdims=True) + RMS_EPS) * gain * (1.0 - lam_init)
    return o


def _group_start(g, tk):
    return pl.multiple_of(N_META + g * tk, 16)


DIFF_HEADS_PER_STEP = 2


def _diff_prompt_body(q_ref, k_ref, v_ref, lq1, lk1, lq2, lk2, g_ref, o_ref, *, lam_init, tq, tk):
    i = pl.program_id(2)
    nh = q_ref.shape[2] // LANES
    lane = lax.broadcasted_iota(I32, (tq, LANES), 1)
    qq = []
    for h in range(nh):
        q = q_ref[0, :, h * LANES:(h + 1) * LANES].astype(F32)
        qq.append(jnp.concatenate([jnp.where(lane < DH_A, q, 0.0), jnp.where(lane >= DH_A, q, 0.0)],
                                  axis=0).astype(BF16))

    def step(start, nk, mask, carry):
        kc = k_ref[0, pl.ds(start, nk), :]
        vc = v_ref[0, pl.ds(start, nk), :]
        scores = lambda h: lax.dot_general(qq[h], kc[:, h * LANES:(h + 1) * LANES], NT_DIMS,
                                           preferred_element_type=F32)
        out, s_next = [], scores(0)
        for h in range(nh):
            s, s_next = s_next, (scores(h + 1) if h + 1 < nh else None)
            if mask is not None:
                s = jnp.where(mask, s, -jnp.inf)
            out.append(_softmax_step(s, vc[:, h * LANES:(h + 1) * LANES], *carry[h]))
        return tuple(out)

    carry = tuple((jnp.full((2 * tq, 1), NEG, F32), jnp.zeros((2 * tq, 1), F32), jnp.zeros((2 * tq, LANES), F32))
                  for _ in range(nh))
    carry = step(0, KEY_CHUNK, lax.broadcasted_iota(I32, (2 * tq, KEY_CHUNK), 1) < N_META, carry)
    n_full = (i * tq) // tk
    carry = lax.fori_loop(0, n_full, lambda g, cr: step(_group_start(g, tk), tk, None, cr), carry)
    kx = n_full * tk + lax.broadcasted_iota(I32, (2 * tq, tk), 1)
    qx = i * tq + (lax.broadcasted_iota(I32, (2 * tq, tk), 0) & (tq - 1))
    carry = step(_group_start(n_full, tk), tk, kx <= qx, carry)

    lam = _diff_lambda(lq1, lk1, lq2, lk2, lam_init)
    for h in range(nh):
        _, l, acc = carry[h]
        o = acc[:tq] / l[:tq] - lam * (acc[tq:] / l[tq:])
        o_ref[0, :, h * LANES:(h + 1) * LANES] = _diff_finish(o, g_ref[...], lam_init).astype(o_ref.dtype)


def _diff_prompt(qa, ka, va, lam_vecs, gain, lam_init, tq, tk):
    B, S, _ = qa.shape
    T = ka.shape[1]
    nq = S // tq
    body = functools.partial(_diff_prompt_body, lam_init=lam_init, tq=tq, tk=tk)
    res = lambda b, h, i: (b, 0, h)
    blk = lambda b, h, i: (b, i, h)
    w = DIFF_HEADS_PER_STEP * LANES
    return pl.pallas_call(
        body, grid=(B, H_A // DIFF_HEADS_PER_STEP, nq),
        in_specs=[pl.BlockSpec((1, tq, w), blk),
                  pl.BlockSpec((1, T, w), res), pl.BlockSpec((1, T, w), res)]
                 + [_full_spec(v) for v in lam_vecs] + [_full_spec(gain)],
        out_specs=pl.BlockSpec((1, tq, w), blk),
        out_shape=jax.ShapeDtypeStruct((B, S, W_A_DIM), BF16),
        compiler_params=_cparams(("parallel", "parallel", "arbitrary")), name="diff_prompt",
    )(qa, ka, va, *lam_vecs, gain)


def _sort_key(x):
    b = lax.bitcast_convert_type(x, I32)
    b = jnp.where((b & 0x7F800000) == 0, 0, b)
    hi = lax.bitcast_convert_type(b & jnp.int32(-65536), F32).astype(BF16)
    return b ^ ((b >> 31) & 0x7FFFFFFF), hi


COUNT_ROWS = 128


def _count_rows(keys_ref, ngrp, cpg, pred, *operands):
    rows = keys_ref.shape[0]
    rb = min(rows, COUNT_ROWS)
    dt = keys_ref.dtype
    one_, zero_ = jnp.ones((), dt), jnp.zeros((), dt)
    accs = []
    for r0 in range(0, rows, rb):
        ops = [o[r0:r0 + rb] for o in operands]

        def one(col0, acc):
            kc = keys_ref[r0:r0 + rb, pl.ds(col0, LANES)]
            return acc + jnp.where(pred(kc, col0, *ops), one_, zero_)

        def grp(g, acc):
            base = pl.multiple_of(LANES + g * (cpg * LANES), LANES)
            for j in range(cpg):
                acc = one(base + j * LANES, acc)
            return acc

        accs.append(lax.fori_loop(0, ngrp, grp, one(0, jnp.zeros((rb, LANES), dt))))
    acc = accs[0] if len(accs) == 1 else jnp.concatenate(accs, axis=0)
    cnt = jnp.sum(acc.astype(F32), axis=1, keepdims=True)
    return jnp.broadcast_to(cnt, (rows, LANES))


def _select_threshold(keys_ref, ngrp, cpg, k, hi_ref=None):
    rows = keys_ref.shape[0]

    def take_bit(t, carry, count):
        prefix, n_ge = carry
        cand = prefix ^ lax.shift_left(jnp.int32(1), jnp.int32(31) - t)
        cnt = count(cand)
        take = cnt >= k
        return jnp.where(take, cand, prefix), jnp.where(take, cnt, n_ge)

    def count_keys(cand):
        return _count_rows(keys_ref, ngrp, cpg, lambda kc, c0, cd: kc >= cd, cand)

    def count_hi(cand):
        c16 = cand >> 16
        pat = (c16 ^ ((c16 >> 15) & 0x7FFF)) & 0xFFFF
        subnormal = ((pat & 0x7F80) == 0) & ((pat & 0x7F) != 0)
        pat = jnp.where(subnormal, jnp.where(pat < 0x8000, 0x0080, 0), pat)
        cf = lax.bitcast_convert_type(lax.shift_left(pat, 16), F32).astype(BF16)
        return _count_rows(hi_ref, ngrp, cpg, lambda hc, c0, cd: hc >= cd, cf)

    carry = (jnp.full((rows, LANES), INT_MIN, I32), jnp.zeros((rows, LANES), F32))
    n_hi = 0
    if hi_ref is not None:
        n_hi = 16
        carry = lax.fori_loop(0, n_hi, lambda t, cr: take_bit(t, cr, count_hi), carry)
    tau, n_ge = lax.fori_loop(n_hi, 32, lambda t, cr: take_bit(t, cr, count_keys), carry)
    tau = jnp.maximum(tau, INT_MIN + 1)
    surplus = jnp.max(jnp.maximum(n_ge - k, 0.0))

    @pl.when(surplus > 0.0)
    def _():
        n_gt = _count_rows(keys_ref, ngrp, cpg, lambda kc, c0, tu: kc > tu, tau)
        allowed = k - n_gt
        lane = lax.broadcasted_iota(I32, (rows, LANES), 1)
        nbits = max(1, int(math.ceil(math.log2(keys_ref.shape[1] + 1))))

        def col_body(t, lim):
            cand = lim | lax.shift_left(jnp.int32(1), jnp.int32(nbits - 1) - t)
            cnt = _count_rows(
                keys_ref, ngrp, cpg,
                lambda kc, c0, tu, cd: (kc == tu) & (lax.broadcasted_iota(I32, kc.shape, 1) + c0 < cd), tau, cand)
            return jnp.where(cnt < allowed, cand, lim)

        lim = lax.fori_loop(0, nbits, col_body, jnp.zeros((rows, LANES), I32))

        def drop(c, carry):
            c0 = pl.multiple_of(c * LANES, LANES)
            kc = keys_ref[:, pl.ds(c0, LANES)]
            keys_ref[:, pl.ds(c0, LANES)] = jnp.where((kc == tau) & (lane + c0 > lim), INT_MIN, kc)
            return carry

        lax.fori_loop(0, 1 + ngrp * cpg, drop, 0)

    return tau


SCORE_BLOCK = 256


def _masked_softmax_step(s, sel, v, c2, m, l, acc):
    s = jnp.where(sel, s, -jnp.inf)
    m_new = jnp.maximum(m, jnp.max(s, axis=1, keepdims=True))
    alpha = jnp.exp2((m - m_new) * c2)
    p = jnp.exp2((s - m_new) * c2)
    l = alpha * l + jnp.sum(p, axis=1, keepdims=True)
    acc = alpha * acc + jnp.dot(p.astype(BF16), v, preferred_element_type=F32)
    return m_new, l, acc


def _dsa_prompt_body(qb_ref, qi_ref, wi_ref, ki_ref, kb_ref, vb_ref, o_ref, keys_ref, hi_ref, *, tq, tk, topk):
    i = pl.program_id(1)
    n_full = (i * tq) // tk
    ngrp = n_full + 1
    cpg = tk // LANES
    lane = lax.broadcasted_iota(I32, (tq, LANES), 1)
    row = lax.broadcasted_iota(I32, (tq, LANES), 0)

    qi = qi_ref[0]
    wi = wi_ref[0]
    qst = jnp.concatenate([qi[:, h * D_I:(h + 1) * D_I] for h in range(H_I)], axis=0)
    wh = [jnp.broadcast_to(wi[:, h:h + 1], (tq, LANES)) for h in range(H_I)]

    def score(start, nk, col0, x0):
        kc = ki_ref[0, pl.ds(start, nk), :]
        r = lax.dot_general(qst, kc, NT_DIMS, preferred_element_type=F32)
        for j in range(nk // LANES):
            sc = jnp.zeros((tq, LANES), F32)
            for h in range(H_I):
                sc = sc + wh[h] * jnp.maximum(r[h * tq:(h + 1) * tq, j * LANES:(j + 1) * LANES], 0.0)
            key, hi = _sort_key(sc)
            if x0 is not False:
                valid = (lane < N_META) if x0 is None else (x0 + j * LANES + lane <= i * tq + row)
                key = jnp.where(valid, key, INT_MIN)
                hi = jnp.where(valid, hi.astype(F32), jnp.nan).astype(BF16)
            keys_ref[:, pl.ds(col0 + j * LANES, LANES)] = key
            hi_ref[:, pl.ds(col0 + j * LANES, LANES)] = hi

    score(0, KEY_CHUNK, 0, None)

    def score_group(g, masked):
        sb = min(SCORE_BLOCK, tk)
        for jb in range(tk // sb):
            off = g * tk + jb * sb
            score(pl.multiple_of(N_META + off, 16), sb, pl.multiple_of(LANES + off, LANES),
                  off if masked else False)

    def score_loop(g, carry):
        score_group(g, False)
        return carry

    lax.fori_loop(0, n_full, score_loop, 0)
    score_group(n_full, True)

    tau = _select_threshold(keys_ref, ngrp, cpg, topk, hi_ref)

    q = qb_ref[0]
    qs = [q[:, h * DH_B:(h + 1) * DH_B] for h in range(H_B)]
    c2 = (DH_B ** -0.5) * math.log2(math.e)

    def att(start, nk, col0, carry):
        taub = tau if nk == LANES else jnp.concatenate([tau] * (nk // LANES), axis=1)
        sel = keys_ref[:, pl.ds(col0, nk)] >= taub
        kc = kb_ref[0, pl.ds(start, nk), :]
        vc = vb_ref[0, pl.ds(start, nk), :]
        hs = [slice(h * DH_B, (h + 1) * DH_B) for h in range(H_B)]
        scores = lambda h: lax.dot_general(qs[h], kc[:, hs[h]], NT_DIMS, preferred_element_type=F32)
        out, s_next = [], scores(0)
        for h in range(H_B):
            s, s_next = s_next, (scores(h + 1) if h + 1 < H_B else None)
            out.append(_masked_softmax_step(s, sel, vc[:, hs[h]], c2, *carry[h]))
        return tuple(out)

    init = tuple((jnp.full((tq, 1), NEG, F32), jnp.zeros((tq, 1), F32), jnp.zeros((tq, DH_B), F32))
                 for _ in range(H_B))
    carry = att(0, KEY_CHUNK, 0, init)
    carry = lax.fori_loop(
        0, ngrp, lambda g, cr: att(_group_start(g, tk), tk, pl.multiple_of(LANES + g * tk, LANES), cr), carry)
    for h in range(H_B):
        _, l, acc = carry[h]
        o_ref[0, :, h * DH_B:(h + 1) * DH_B] = (acc / l).astype(o_ref.dtype)


def _dsa_prompt(qb, qi, wi, ki, kb, vb, topk, tq, tk):
    B, S, _ = qb.shape
    T = kb.shape[1]
    nq = S // tq
    body = functools.partial(_dsa_prompt_body, tq=tq, tk=tk, topk=topk)
    blk = lambda b, i: (b, i, 0)
    res = lambda b, i: (b, 0, 0)
    once = pl.Buffered(1)
    return pl.pallas_call(
        body, grid=(B, nq),
        in_specs=[pl.BlockSpec((1, tq, W_B_DIM), blk), pl.BlockSpec((1, tq, H_I * D_I), blk),
                  pl.BlockSpec((1, tq, H_I), blk),
                  pl.BlockSpec((1, T, D_I), res, pipeline_mode=once),
                  pl.BlockSpec((1, T, W_B_DIM), res, pipeline_mode=once),
                  pl.BlockSpec((1, T, W_B_DIM), res, pipeline_mode=once)],
        out_specs=pl.BlockSpec((1, tq, W_B_DIM), blk),
        out_shape=jax.ShapeDtypeStruct((B, S, W_B_DIM), BF16),
        scratch_shapes=[pltpu.VMEM((tq, LANES + S), I32), pltpu.VMEM((tq, LANES + S), BF16)],
        compiler_params=_cparams(("parallel", "arbitrary")), name="dsa_prompt",
    )(qb, qi, wi, ki, kb, vb)


def _div(x, d):
    assert d & (d - 1) == 0
    return x >> (d.bit_length() - 1)


def _mod(x, d):
    assert d & (d - 1) == 0
    return x & (d - 1)


def _pad_rows(a, rows):
    return jnp.concatenate([a, jnp.zeros((rows - a.shape[0], a.shape[1]), a.dtype)], axis=0)


def _page_specs(n_pages, rows, width):
    return [pl.BlockSpec((rows, width), functools.partial(lambda b, pt, p: (pt[b, p], 0), p=p))
            for p in range(n_pages)]


def _head_rows(q, heads):
    return jnp.concatenate([q[:, h * LANES:(h + 1) * LANES] for h in range(heads)], axis=0)


def _diff_sample_body(pt_ref, q_ref, kn_ref, vn_ref, *rest, n_pages, lam_init, ts):
    k_pages = rest[:n_pages]
    v_pages = rest[n_pages:2 * n_pages]
    lq1, lk1, lq2, lk2, g_ref, o_ref = rest[2 * n_pages:]
    lam = _diff_lambda(lq1, lk1, lq2, lk2, lam_init)
    rows = H_A * ts
    pw = KEY_CHUNK * H_A
    qh = _head_rows(q_ref[0].astype(F32), H_A)
    lane = lax.broadcasted_iota(I32, (rows, LANES), 1)
    qq = jnp.concatenate([jnp.where(lane < DH_A, qh, 0.0), jnp.where(lane >= DH_A, qh, 0.0)], axis=0).astype(BF16)
    kn = _pad_rows(kn_ref[0], LANES).astype(BF16)
    vn = _pad_rows(vn_ref[0], LANES).astype(BF16)
    r2 = lax.broadcasted_iota(I32, (2 * rows, pw), 0)
    c2 = lax.broadcasted_iota(I32, (2 * rows, pw), 1)
    same_head = _mod(_div(r2, ts), H_A) == _mod(c2, H_A)
    rn = lax.broadcasted_iota(I32, (2 * rows, LANES), 0)
    cn = lax.broadcasted_iota(I32, (2 * rows, LANES), 1)
    new_ok = (_mod(_div(rn, ts), H_A) == _mod(cn, H_A)) & (_div(cn, H_A) <= _mod(rn, ts)) & (cn < ts * H_A)
    s = jnp.concatenate(
        [jnp.where(same_head, lax.dot_general(qq, k_pages[p][...].astype(BF16), NT_DIMS,
                                              preferred_element_type=F32), -jnp.inf) for p in range(n_pages)]
        + [jnp.where(new_ok, lax.dot_general(qq, kn, NT_DIMS, preferred_element_type=F32), -jnp.inf)], axis=1)
    m = jnp.max(s, axis=1, keepdims=True)
    pr = jnp.exp(s - m)
    pr = pr / jnp.sum(pr, axis=1, keepdims=True)
    wgt = (pr[:rows] - lam * pr[rows:]).astype(BF16)
    o = jnp.dot(wgt[:, n_pages * pw:], vn, preferred_element_type=F32)
    for p in range(n_pages):
        o = o + jnp.dot(wgt[:, p * pw:(p + 1) * pw], v_pages[p][...].astype(BF16), preferred_element_type=F32)
    o = _diff_finish(o, g_ref[...], lam_init)
    for h in range(H_A):
        o_ref[0, :, h * LANES:(h + 1) * LANES] = o[h * ts:(h + 1) * ts].astype(o_ref.dtype)


def _diff_sample(qa, ka_new, va_new, pool_k, pool_v, page_table, lam_vecs, gain, lam_init):
    Bd, ts, _ = qa.shape
    n_pages = page_table.shape[1]
    pw = KEY_CHUNK * H_A
    body = functools.partial(_diff_sample_body, n_pages=n_pages, lam_init=lam_init, ts=ts)
    blk = lambda b, pt: (b, 0, 0)
    const2 = lambda b, pt: (0, 0)
    new_spec = pl.BlockSpec((1, ts * H_A, LANES), blk)
    grid_spec = pltpu.PrefetchScalarGridSpec(
        num_scalar_prefetch=1, grid=(Bd,),
        in_specs=[pl.BlockSpec((1, ts, W_A_DIM), blk), new_spec, new_spec]
                 + _page_specs(n_pages, pw, LANES) + _page_specs(n_pages, pw, LANES)
                 + [pl.BlockSpec(v.shape, const2) for v in lam_vecs] + [pl.BlockSpec(gain.shape, const2)],
        out_specs=pl.BlockSpec((1, ts, W_A_DIM), blk))
    return pl.pallas_call(
        body, grid_spec=grid_spec, out_shape=jax.ShapeDtypeStruct((Bd, ts, W_A_DIM), BF16),
        compiler_params=_cparams(("arbitrary",)), name="diff_sample",
    )(page_table, qa, ka_new, va_new, *([pool_k] * n_pages), *([pool_v] * n_pages), *lam_vecs, gain)


def _dsa_sample_body(pt_ref, qb_ref, qi_ref, wi_ref, kin_ref, kn_ref, vn_ref, *rest, n_pages, ts, topk, cpg):
    ki_pages = rest[:n_pages]
    k_pages = rest[n_pages:2 * n_pages]
    v_pages = rest[2 * n_pages:3 * n_pages]
    o_ref, keys_ref = rest[3 * n_pages:]
    rows_i = H_I * ts
    qi = qi_ref[0]
    wcol = jnp.broadcast_to(wi_ref[0], (rows_i, LANES))
    lane = lax.broadcasted_iota(I32, (ts, LANES), 1)
    row = lax.broadcasted_iota(I32, (ts, LANES), 0)

    def idx_scores(kc):
        r = lax.dot_general(qi, kc, NT_DIMS, preferred_element_type=F32)
        r = wcol * jnp.maximum(r, 0.0)
        sc = r[0:ts]
        for h in range(1, H_I):
            sc = sc + r[h * ts:(h + 1) * ts]
        return _sort_key(sc)[0]

    for p in range(n_pages):
        keys_ref[:, p * LANES:(p + 1) * LANES] = idx_scores(ki_pages[p][0].astype(BF16))
    kin = _pad_rows(kin_ref[0], KEY_CHUNK).astype(BF16)
    keys_ref[:, n_pages * LANES:] = jnp.where((lane <= row) & (lane < ts), idx_scores(kin), INT_MIN)

    tau = _select_threshold(keys_ref, n_pages // cpg, cpg, topk)

    pw = KEY_CHUNK * H_B
    rows = H_B * ts
    expand = jnp.where(_div(lax.broadcasted_iota(I32, (KEY_CHUNK, pw), 1), H_B)
                       == lax.broadcasted_iota(I32, (KEY_CHUNK, pw), 0), 1.0, 0.0).astype(BF16)
    r2 = lax.broadcasted_iota(I32, (rows, pw), 0)
    c2i = lax.broadcasted_iota(I32, (rows, pw), 1)
    same_head = _div(r2, ts) == _mod(c2i, H_B)

    def selected(p):
        sel = jnp.where(keys_ref[:, p * LANES:(p + 1) * LANES] >= tau, 1.0, 0.0)
        sel = jnp.dot(_pad_rows(sel, 2 * ts).astype(BF16), expand, preferred_element_type=F32)[:ts]
        return same_head & (jnp.concatenate([sel] * H_B, axis=0) > 0.5)

    qh = _head_rows(qb_ref[0].astype(F32), H_B).astype(BF16)
    kn = _pad_rows(kn_ref[0], LANES).astype(BF16)
    vn = _pad_rows(vn_ref[0], LANES).astype(BF16)
    c2 = (DH_B ** -0.5) * math.log2(math.e)
    s = jnp.concatenate(
        [jnp.where(selected(p), lax.dot_general(qh, k_pages[p][...].astype(BF16), NT_DIMS,
                                                preferred_element_type=F32), -jnp.inf) for p in range(n_pages)]
        + [jnp.where(selected(n_pages)[:, :LANES], lax.dot_general(qh, kn, NT_DIMS, preferred_element_type=F32),
                     -jnp.inf)], axis=1)
    m = jnp.max(s, axis=1, keepdims=True)
    pr = jnp.exp2((s - m) * c2)
    pr = (pr / jnp.sum(pr, axis=1, keepdims=True)).astype(BF16)
    o = jnp.dot(pr[:, n_pages * pw:], vn, preferred_element_type=F32)
    for p in range(n_pages):
        o = o + jnp.dot(pr[:, p * pw:(p + 1) * pw], v_pages[p][...].astype(BF16), preferred_element_type=F32)
    for h in range(H_B):
        o_ref[0, :, h * LANES:(h + 1) * LANES] = o[h * ts:(h + 1) * ts].astype(o_ref.dtype)


def _dsa_sample(qb, qi_hs, wi_hs, ki_new, kb_new, vb_new, pool_ki, pool_k, pool_v, page_table, topk):
    Bd, ts, _ = qb.shape
    n_pages = page_table.shape[1]
    pw = KEY_CHUNK * H_B
    cpg = 4 if n_pages % 4 == 0 else 1
    body = functools.partial(_dsa_sample_body, n_pages=n_pages, ts=ts, topk=topk, cpg=cpg)
    blk = lambda b, pt: (b, 0, 0)
    new_spec = pl.BlockSpec((1, ts * H_B, LANES), blk)
    ki_specs = [pl.BlockSpec((1, KEY_CHUNK, D_I), functools.partial(lambda b, pt, p: (pt[b, p], 0, 0), p=p))
                for p in range(n_pages)]
    grid_spec = pltpu.PrefetchScalarGridSpec(
        num_scalar_prefetch=1, grid=(Bd,),
        in_specs=[pl.BlockSpec((1, ts, W_B_DIM), blk), pl.BlockSpec((1, H_I * ts, D_I), blk),
                  pl.BlockSpec((1, H_I * ts, 1), blk), pl.BlockSpec((1, ts, D_I), blk), new_spec, new_spec]
                 + ki_specs + _page_specs(n_pages, pw, LANES) + _page_specs(n_pages, pw, LANES),
        out_specs=pl.BlockSpec((1, ts, W_B_DIM), blk),
        scratch_shapes=[pltpu.VMEM((ts, (n_pages + 1) * LANES), I32)])
    return pl.pallas_call(
        body, grid_spec=grid_spec, out_shape=jax.ShapeDtypeStruct((Bd, ts, W_B_DIM), BF16),
        compiler_params=_cparams(("arbitrary",)), name="dsa_sample",
    )(page_table, qb, qi_hs, wi_hs, ki_new, kb_new, vb_new,
      *([pool_ki] * n_pages), *([pool_k] * n_pages), *([pool_v] * n_pages))


def _layer_norm(h, g, b):
    mu = jnp.mean(h, axis=1, keepdims=True)
    d = h - mu
    var = jnp.mean(d * d, axis=1, keepdims=True)
    return d * lax.rsqrt(var + LN_EPS) * g + b


def _sigmoid(x):
    return 1.0 / (1.0 + jnp.exp(-x))


def _route(logits):
    tm = logits.shape[0]
    lane_i = lax.broadcasted_iota(I32, (tm, LANES), 1)
    lane = lane_i.astype(F32)
    big = float(LANES)
    is_grp = (lane_i >= N_EXPERTS) & (lane_i < N_EXPERTS + N_GROUPS)
    lg = jnp.where(is_grp, logits, -jnp.inf)
    gmax = jnp.max(lg, axis=1, keepdims=True)
    g_val = 1.0 / jnp.sum(jnp.exp(lg - gmax), axis=1, keepdims=True)
    g_idx = jnp.min(jnp.where(lg == gmax, lane - N_EXPERTS, big), axis=1, keepdims=True)
    grp_of_lane = (lane_i >> int(math.log2(EXP_PER_GROUP))).astype(F32)
    in_grp = (lane_i < N_EXPERTS) & (grp_of_lane == g_idx)
    le = jnp.where(in_grp, logits, -jnp.inf)
    m1 = jnp.max(le, axis=1, keepdims=True)
    i1 = jnp.min(jnp.where(le == m1, lane, big), axis=1, keepdims=True)
    le2 = jnp.where(lane == i1, -jnp.inf, le)
    m2 = jnp.max(le2, axis=1, keepdims=True)
    i2 = jnp.min(jnp.where(le2 == m2, lane, big), axis=1, keepdims=True)
    e2 = jnp.exp(m2 - m1)
    gate1 = g_val / (1.0 + e2)
    gate2 = g_val * e2 / (1.0 + e2)
    return jnp.where(lane == i1, gate1, 0.0) + jnp.where(lane == i2, gate2, 0.0)


def _merge_body(x_ref, oa_ref, ob_ref, wa_ref, wb_ref, wg_ref, wo_ref, g1_ref, b1_ref, wr_ref, br_ref,
                x1_ref, comb_ref, *, alpha):
    x = x_ref[...]
    d = x.shape[1]
    ya = jnp.dot(oa_ref[...], wa_ref[...], preferred_element_type=F32)
    yb = jnp.dot(ob_ref[...], wb_ref[...], preferred_element_type=F32)
    gates = jnp.dot(x.astype(BF16), wg_ref[...], preferred_element_type=F32)
    m = _sigmoid(gates[:, :d]) * ya + _sigmoid(gates[:, d:]) * yb
    mix = jnp.dot(m.astype(BF16), wo_ref[...], preferred_element_type=F32)
    x1 = _layer_norm(alpha * x + mix, g1_ref[...], b1_ref[...])
    x1_ref[...] = x1
    logits = jnp.dot(x1, wr_ref[...], preferred_element_type=F32, precision=lax.Precision.HIGHEST) + br_ref[...]
    comb_ref[...] = _route(logits)


def _merge(x, oa, ob, wa, wb, wg, wo, g1, b1, wr, br, alpha, tm):
    n, d = x.shape
    body = functools.partial(_merge_body, alpha=alpha)
    rowblk = lambda w: pl.BlockSpec((tm, w), lambda i: (i, 0))
    consts = [wa, wb, wg, wo, g1, b1, wr, br]
    return pl.pallas_call(
        body, grid=(n // tm,),
        in_specs=[rowblk(d), rowblk(oa.shape[1]), rowblk(ob.shape[1])] + [_full_spec(c) for c in consts],
        out_specs=[rowblk(d), rowblk(LANES)],
        out_shape=[jax.ShapeDtypeStruct((n, d), F32), jax.ShapeDtypeStruct((n, LANES), F32)],
        compiler_params=_cparams(("parallel",)), name="merge_ln1_route",
    )(x, oa, ob, *consts)


def _moe_body(x_ref, comb_ref, wg_ref, wu_ref, wd_ref, g2_ref, b2_ref, y_ref, xb_ref, acc_ref, *, alpha):
    e = pl.program_id(1)

    @pl.when(e == 0)
    def _():
        xb_ref[...] = x_ref[...].astype(BF16)
        acc_ref[...] = jnp.zeros_like(acc_ref)

    xb = xb_ref[...]
    hg = jnp.dot(xb, wg_ref[0], preferred_element_type=F32)
    hu = jnp.dot(xb, wu_ref[0], preferred_element_type=F32)
    comb = comb_ref[...]
    lane = lax.broadcasted_iota(I32, comb.shape, 1)
    ce = jnp.sum(jnp.where(lane == e, comb, 0.0), axis=1, keepdims=True)
    act = hg * _sigmoid(hg) * hu * ce
    acc_ref[...] += jnp.dot(act.astype(BF16), wd_ref[0], preferred_element_type=F32)

    @pl.when(e == pl.num_programs(1) - 1)
    def _():
        y_ref[...] = _layer_norm(alpha * x_ref[...] + acc_ref[...], g2_ref[...], b2_ref[...])


def _moe(x1, comb, wg, wu, wd, g2, b2, alpha, tm):
    n, d = x1.shape
    ne, _, dff = wg.shape
    body = functools.partial(_moe_body, alpha=alpha)
    return pl.pallas_call(
        body, grid=(n // tm, ne),
        in_specs=[pl.BlockSpec((tm, d), lambda i, e: (i, 0)), pl.BlockSpec((tm, LANES), lambda i, e: (i, 0)),
                  pl.BlockSpec((1, d, dff), lambda i, e: (e, 0, 0)), pl.BlockSpec((1, d, dff), lambda i, e: (e, 0, 0)),
                  pl.BlockSpec((1, dff, d), lambda i, e: (e, 0, 0)),
                  pl.BlockSpec(g2.shape, lambda i, e: (0, 0)), pl.BlockSpec(b2.shape, lambda i, e: (0, 0))],
        out_specs=pl.BlockSpec((tm, d), lambda i, e: (i, 0)),
        out_shape=jax.ShapeDtypeStruct((n, d), F32),
        scratch_shapes=[pltpu.VMEM((tm, d), BF16), pltpu.VMEM((tm, d), F32)],
        compiler_params=_cparams(("parallel", "arbitrary")), name="moe_ln2",
    )(x1, comb, wg, wu, wd, g2, b2)


def _row_tile(n, target):
    t = min(n, target)
    while n % t:
        t //= 2
    return t


def kernel(x_prompt, x_sample, cache_diff_k, cache_diff_v, cache_dsa_k, cache_dsa_v, cache_idx_k, page_table, meta_tokens, w_in, lam_q1, lam_k1, lam_q2, lam_k2, diff_norm_g, w_branch_a, w_branch_b, w_out, ln1_g, ln1_b, w_grp, b_grp, w_exp, b_exp, w_gate, w_up, w_down, ln2_g, ln2_b):
    depth = w_in.shape[0]
    assert depth == 1, "single-layer step only"
    B, S, D = x_prompt.shape
    Bd, Ts, _ = x_sample.shape
    T = S + N_META
    n_pool, page = cache_diff_k.shape[1], cache_diff_k.shape[2]
    assert page == KEY_CHUNK and S % KEY_CHUNK == 0 and Ts % 8 == 0
    n_pages = page_table.shape[1]
    past = n_pages * page
    topk_p = min(TOPK_MAX, S // 4)
    topk_s = min(TOPK_MAX, (past + Ts) // 4)
    alpha = (2.0 * depth) ** 0.25
    lam_init = 0.8 - 0.6 * math.exp(-0.3 * 0)

    w = w_in[0]
    widths = list(IN_WIDTHS[:9]) + [D, D]
    cuts = np.concatenate([[0], np.cumsum(widths)])
    w_qa, w_ka, w_va, w_qb, w_kb, w_vb, w_qi, w_ki, w_wi, w_ga, w_gb = [
        w[:, int(cuts[j]):int(cuts[j + 1])].astype(BF16) for j in range(11)]
    w_ki = jnp.pad(w_ki, ((0, 0), (0, LANES - D_I)))
    w_wi = jnp.pad(w_wi, ((0, 0), (0, LANES - H_I)))
    w_g = jnp.concatenate([w_ga, w_gb], axis=1)
    lam_vecs = [v.astype(F32) for v in (lam_q1, lam_k1, lam_q2, lam_k2)]
    gain = diff_norm_g.astype(F32)
    w_r = jnp.pad(jnp.concatenate([w_exp[0], w_grp[0]], axis=1), ((0, 0), (0, LANES - N_EXPERTS - N_GROUPS)))
    b_r = jnp.pad(jnp.concatenate([b_exp[0], b_grp[0]])[None, :], ((0, 0), (0, LANES - N_EXPERTS - N_GROUPS)))
    wa16, wb16, wo16 = w_branch_a[0].astype(BF16), w_branch_b[0].astype(BF16), w_out[0].astype(BF16)
    wg16, wu16, wd16 = w_gate[0].astype(BF16), w_up[0].astype(BF16), w_down[0].astype(BF16)

    k_segs = [(w_ka, "A", [(F32, W_A_DIM, 1.0, H_A), (BF16, W_A_DIM, 1.0, 0)]),
              (w_va, None, [(F32, W_A_DIM, 1.0, H_A), (BF16, W_A_DIM, 1.0, 0)]),
              (w_kb, "B", [(F32, W_B_DIM, 1.0, H_B), (BF16, W_B_DIM, 1.0, 0)]),
              (w_vb, None, [(F32, W_B_DIM, 1.0, H_B), (BF16, W_B_DIM, 1.0, 0)]),
              (w_ki, "A", [(F32, D_I, 1.0, 0), (BF16, D_I, 1.0, 0)])]
    q_segs = [(w_qa, "A", [(BF16, W_A_DIM, DH_A ** -0.5, 0)]),
              (w_qb, "B", [(BF16, W_B_DIM, 1.0, 0)]),
              (w_qi, "A", [(BF16, H_I * D_I, D_I ** -0.5, 0)]),
              (w_wi, None, [(F32, H_I, H_I ** -0.5, 0)])]

    pos_p = jnp.arange(T, dtype=jnp.int32)
    tabs_p = _rope_tables(pos_p, DH_A // 4, DH_A) + _rope_tables(pos_p, DH_B // 4, DH_B)
    tabs_q = tuple(t[N_META:] for t in tabs_p)
    meta = jnp.broadcast_to(meta_tokens[None].astype(x_prompt.dtype), (B, N_META, D))
    xp = jnp.concatenate([meta, x_prompt], axis=1)
    (ka_p, ka16, va_p, va16, kb_p, kb16, vb_p, vb16, ki_p, ki16) = _project(xp, tabs_p, k_segs, 512, "proj_k_prompt")
    qa16, qb16, qi16, wi_p = _project(x_prompt, tabs_q, q_segs, _row_tile(S, 512), "proj_q_prompt")
    tk = next(t for t in (512, 256, KEY_CHUNK) if S % t == 0)
    tq = min(256, tk)
    oa_p = _diff_prompt(qa16, ka16, va16, lam_vecs, gain, lam_init, tq, tk)
    ob_p = _dsa_prompt(qb16, qi16, wi_p, ki16, kb16, vb16, topk_p, tq, tk)

    ns = Bd * Ts
    pos_s = past + (jnp.arange(ns, dtype=jnp.int32) % Ts)
    tabs_s = _rope_tables(pos_s, DH_A // 4, DH_A) + _rope_tables(pos_s, DH_B // 4, DH_B)
    s_segs = [(wt, rope, [o for o in outs if o[0] == F32]) for (wt, rope, outs) in k_segs] + q_segs
    (ka_s, va_s, kb_s, vb_s, ki_s, qa_s, qb_s, qi_s, wi_s) = _project(
        x_sample.reshape(1, ns, D), tabs_s, s_segs, _row_tile(ns, 512), "proj_sample")
    seq = lambda a: a.reshape(Bd, -1, a.shape[-1])
    pool = lambda c: c.reshape(-1, c.shape[-1])
    oa_s = _diff_sample(seq(qa_s), seq(ka_s), seq(va_s), pool(cache_diff_k), pool(cache_diff_v),
                        page_table, lam_vecs, gain, lam_init)
    qi_hs = seq(qi_s).reshape(Bd, Ts, H_I, D_I).transpose(0, 2, 1, 3).reshape(Bd, H_I * Ts, D_I)
    wi_hs = seq(wi_s).transpose(0, 2, 1).reshape(Bd, H_I * Ts, 1)
    ob_s = _dsa_sample(seq(qb_s), qi_hs, wi_hs, seq(ki_s), seq(kb_s), seq(vb_s),
                       cache_idx_k.reshape(n_pool, page, D_I), pool(cache_dsa_k), pool(cache_dsa_v), page_table, topk_s)

    def tail(x2d, oa2d, ob2d):
        n = x2d.shape[0]
        x1, comb = _merge(x2d, oa2d, ob2d, wa16, wb16, w_g, wo16, ln1_g, ln1_b, w_r, b_r, alpha, _row_tile(n, 512))
        return _moe(x1, comb, wg16, wu16, wd16, ln2_g, ln2_b, alpha, _row_tile(n, 1024))

    y_prompt = tail(x_prompt.reshape(B * S, D), oa_p.reshape(B * S, -1), ob_p.reshape(B * S, -1)).reshape(B, S, D)
    y_sample = tail(x_sample.reshape(ns, D), oa_s.reshape(ns, -1), ob_s.reshape(ns, -1)).reshape(Bd, Ts, D)

    cache_p = lambda a, h: a.reshape(1, B, T, h, LANES)
    cache_s = lambda a, h: a.reshape(1, Bd, Ts, h, LANES)
    return (y_prompt, y_sample,
            cache_p(ka_p, H_A), cache_p(va_p, H_A), cache_p(kb_p, H_B), cache_p(vb_p, H_B), ki_p[None],
            cache_s(ka_s, H_A), cache_s(va_s, H_A), cache_s(kb_s, H_B), cache_s(vb_s, H_B), seq(ki_s)[None])
```

```python
import functools
import math

import numpy as np
import jax
import jax.numpy as jnp
from jax import lax
from jax.experimental import pallas as pl
from jax.experimental.pallas import tpu as pltpu

F32 = jnp.float32
BF16 = jnp.bfloat16
I32 = jnp.int32

N_META = 16
H_A = 4
DH_A = 64
H_B = 4
DH_B = 128
H_I = 8
D_I = 64
TOPK_MAX = 256
ROPE_THETA = 500000.0
N_GROUPS = 4
EXP_PER_GROUP = 8
N_EXPERTS = N_GROUPS * EXP_PER_GROUP
LN_EPS = 1e-5
RMS_EPS = 1e-6
W_A_DIM = H_A * 2 * DH_A
W_B_DIM = H_B * DH_B
IN_WIDTHS = (W_A_DIM, W_A_DIM, W_A_DIM, W_B_DIM, W_B_DIM, W_B_DIM, H_I * D_I, D_I, H_I, None, None)

LANES = 128
KEY_CHUNK = 128
NEG = -1e30
INT_MIN = -(2 ** 31)
VMEM_LIMIT = 56 * 1024 * 1024

NT_DIMS = (((1,), (1,)), ((), ()))


def _cparams(sem):
    return pltpu.CompilerParams(dimension_semantics=sem, vmem_limit_bytes=VMEM_LIMIT)


def _full_spec(a):
    nd = a.ndim
    return pl.BlockSpec(a.shape, lambda *_: (0,) * nd)


def _rope_tables(pos, rot, period):
    half = rot // 2
    inv = jnp.power(ROPE_THETA, -jnp.arange(half, dtype=F32) / half)
    ang = pos.astype(F32)[:, None] * inv[None, :]
    cos, sin = jnp.cos(ang), jnp.sin(ang)
    n = pos.shape[0]
    zh = jnp.zeros((n, half), F32)
    rest0 = jnp.zeros((n, period - rot), F32)
    c = jnp.concatenate([cos, cos, jnp.ones((n, period - rot), F32)], axis=1)
    sn = jnp.concatenate([-sin, zh, rest0], axis=1)
    sp = jnp.concatenate([zh, sin, rest0], axis=1)
    reps = LANES // period
    return tuple(jnp.tile(t, (1, reps)) for t in (c, sn, sp))


def _rope(y, c, sn, sp, half):
    outs = []
    for j in range(y.shape[1] // LANES):
        ys = y[:, j * LANES:(j + 1) * LANES]
        outs.append(ys * c + pltpu.roll(ys, LANES - half, 1) * sn + pltpu.roll(ys, half, 1) * sp)
    return outs[0] if len(outs) == 1 else jnp.concatenate(outs, axis=1)


def _proj_body(*refs, segs, n_pre):
    if n_pre:
        cur_ref, prev_ref, pre_ref, xs_ref = refs[0], refs[1], refs[2], refs[-1]
        refs = refs[2:-1]
        tm = xs_ref.shape[0]
        xs_ref[n_pre:, :] = cur_ref[0, :tm - n_pre, :]

        @pl.when(pl.program_id(1) == 0)
        def _():
            xs_ref[:n_pre, :] = pre_ref[...]

        @pl.when(pl.program_id(1) > 0)
        def _():
            xs_ref[:n_pre, :] = prev_ref[0, tm - n_pre:, :]

        x = xs_ref[...].astype(BF16)
    else:
        x = refs[0][0].astype(BF16)
    tabs = refs[1:7]
    n = len(segs)
    w_refs = refs[7:7 + n]
    out_refs = refs[7 + n:]
    oi = 0
    for (rope, outs), w_ref in zip(segs, w_refs):
        y = jnp.dot(x, w_ref[...], preferred_element_type=F32)
        if rope == "A":
            y = _rope(y, tabs[0][...], tabs[1][...], tabs[2][...], DH_A // 8)
        elif rope == "B":
            y = _rope(y, tabs[3][...], tabs[4][...], tabs[5][...], DH_B // 8)
        for (_, width, scale, heads) in outs:
            v = y[:, :width]
            if scale != 1.0:
                v = v * scale
            v = v.astype(out_refs[oi].dtype)
            if heads:
                for h in range(heads):
                    out_refs[oi][0, pl.ds(h, v.shape[0], stride=heads), :] = v[:, h * LANES:(h + 1) * LANES]
            else:
                out_refs[oi][0] = v
            oi += 1


def _project(x, tabs, segs, tm, name, prefix=None):
    B, rx, D = x.shape
    n_pre = 0 if prefix is None else prefix.shape[0]
    R = rx + n_pre
    tm = min(tm, R)
    nblk = pl.cdiv(R, tm)
    if n_pre:
        assert rx % tm == 0 and n_pre % 8 == 0 and n_pre < tm
        last = rx // tm - 1
        in_specs = [pl.BlockSpec((1, tm, D), lambda b, i: (b, jnp.minimum(i, last), 0)),
                    pl.BlockSpec((1, tm, D), lambda b, i: (b, jnp.maximum(i - 1, 0), 0)),
                    _full_spec(prefix)]
        inputs = [x, x, prefix]
    else:
        in_specs = [pl.BlockSpec((1, tm, D), lambda b, i: (b, i, 0))]
        inputs = [x]
    in_specs += [pl.BlockSpec((tm, LANES), lambda b, i: (i, 0)) for _ in tabs]
    in_specs += [_full_spec(w) for (w, _, _) in segs]
    out_shape, out_specs = [], []
    for (_, _, outs) in segs:
        for (dt, width, _, heads) in outs:
            shp = (R * heads, LANES) if heads else (R, width)
            blk = (tm * heads, LANES) if heads else (tm, width)
            out_shape.append(jax.ShapeDtypeStruct((B,) + shp, dt))
            out_specs.append(pl.BlockSpec((1,) + blk, lambda b, i: (b, i, 0)))
    body = functools.partial(_proj_body, segs=tuple((rope, tuple(outs)) for (_, rope, outs) in segs), n_pre=n_pre)
    return pl.pallas_call(
        body, grid=(B, nblk), in_specs=in_specs, out_specs=out_specs, out_shape=out_shape,
        scratch_shapes=[pltpu.VMEM((tm, D), F32)] if n_pre else [],
        compiler_params=_cparams(("parallel", "parallel")), name=name,
    )(*inputs, *tabs, *[w for (w, _, _) in segs])


def _softmax_step(s, v, m, l, acc):
    m_new = jnp.maximum(m, jnp.max(s, axis=1, keepdims=True))
    alpha = jnp.exp(m - m_new)
    p = jnp.exp(s - m_new)
    l = alpha * l + jnp.sum(p, axis=1, keepdims=True)
    acc = alpha * acc + jnp.dot(p.astype(BF16), v, preferred_element_type=F32)
    return m_new, l, acc


def _diff_lambda(lq1, lk1, lq2, lk2, lam_init):
    a = jnp.sum(lq1[...] * lk1[...], axis=1, keepdims=True)
    b = jnp.sum(lq2[...] * lk2[...], axis=1, keepdims=True)
    return jnp.exp(a) - jnp.exp(b) + lam_init


def _diff_finish(o, gain, lam_init):
    o = o * lax.rsqrt(jnp.mean(o * o, axis=1, keepdims=True) + RMS_EPS) * gain * (1.0 - lam_init)
    return o


def _group_start(g, tk):
    return pl.multiple_of(N_META + g * tk, 16)


DIFF_HEADS_PER_STEP = 4


def _diff_prompt_body(q_ref, k_ref, v_ref, lq1, lk1, lq2, lk2, g_ref, o_ref, *, lam_init, tq, tk):
    i = pl.program_id(2)
    nh = q_ref.shape[2] // LANES
    lane = lax.broadcasted_iota(I32, (tq, LANES), 1)
    qq = []
    for h in range(nh):
        q = q_ref[0, :, h * LANES:(h + 1) * LANES].astype(F32)
        qq.append(jnp.concatenate([jnp.where(lane < DH_A, q, 0.0), jnp.where(lane >= DH_A, q, 0.0)],
                                  axis=0).astype(BF16))

    def step(start, nk, mask, carry):
        kc = k_ref[0, pl.ds(start, nk), :]
        vc = v_ref[0, pl.ds(start, nk), :]
        scores = lambda h: lax.dot_general(qq[h], kc[:, h * LANES:(h + 1) * LANES], NT_DIMS,
                                           preferred_element_type=F32)
        out, s_next = [], scores(0)
        for h in range(nh):
            s, s_next = s_next, (scores(h + 1) if h + 1 < nh else None)
            if mask is not None:
                s = jnp.where(mask, s, -jnp.inf)
            out.append(_softmax_step(s, vc[:, h * LANES:(h + 1) * LANES], *carry[h]))
        return tuple(out)

    carry = tuple((jnp.full((2 * tq, 1), NEG, F32), jnp.zeros((2 * tq, 1), F32), jnp.zeros((2 * tq, LANES), F32))
                  for _ in range(nh))
    carry = step(0, KEY_CHUNK, lax.broadcasted_iota(I32, (2 * tq, KEY_CHUNK), 1) < N_META, carry)
    n_full = (i * tq) // tk
    carry = lax.fori_loop(0, n_full, lambda g, cr: step(_group_start(g, tk), tk, None, cr), carry)
    kx = n_full * tk + lax.broadcasted_iota(I32, (2 * tq, tk), 1)
    qx = i * tq + (lax.broadcasted_iota(I32, (2 * tq, tk), 0) & (tq - 1))
    carry = step(_group_start(n_full, tk), tk, kx <= qx, carry)

    lam = _diff_lambda(lq1, lk1, lq2, lk2, lam_init)
    for h in range(nh):
        _, l, acc = carry[h]
        o = acc[:tq] / l[:tq] - lam * (acc[tq:] / l[tq:])
        o_ref[0, :, h * LANES:(h + 1) * LANES] = _diff_finish(o, g_ref[...], lam_init).astype(o_ref.dtype)


def _diff_prompt(qa, ka, va, lam_vecs, gain, lam_init, tq, tk):
    B, S, _ = qa.shape
    T = ka.shape[1]
    nq = S // tq
    body = functools.partial(_diff_prompt_body, lam_init=lam_init, tq=tq, tk=tk)
    res = lambda b, h, i: (b, 0, h)
    blk = lambda b, h, i: (b, i, h)
    w = DIFF_HEADS_PER_STEP * LANES
    return pl.pallas_call(
        body, grid=(B, H_A // DIFF_HEADS_PER_STEP, nq),
        in_specs=[pl.BlockSpec((1, tq, w), blk),
                  pl.BlockSpec((1, T, w), res, pipeline_mode=pl.Buffered(1)),
                  pl.BlockSpec((1, T, w), res, pipeline_mode=pl.Buffered(1))]
                 + [_full_spec(v) for v in lam_vecs] + [_full_spec(gain)],
        out_specs=pl.BlockSpec((1, tq, w), blk),
        out_shape=jax.ShapeDtypeStruct((B, S, W_A_DIM), BF16),
        compiler_params=_cparams(("parallel", "parallel", "arbitrary")), name="diff_prompt",
    )(qa, ka, va, *lam_vecs, gain)


MIN_NORMAL = 1.1754943508222875e-38
LOWEST_KEY = INT_MIN + 0x00800000


def _rank_value(x):
    return jnp.where(jnp.abs(x) < MIN_NORMAL, 0.0, x)


def _key_to_float(key):
    b = key ^ ((key >> 31) & 0x7FFFFFFF)
    subnormal = ((b & 0x7F800000) == 0) & ((b & 0x007FFFFF) != 0)
    b = jnp.where(subnormal, jnp.where(b >= 0, 0x00800000, 0), b)
    return lax.bitcast_convert_type(b, F32)


COUNT_ROWS = 128


def _count_rows(sc_ref, ngrp, cpg, pred, *operands):
    rows = sc_ref.shape[0]
    rb = min(rows, COUNT_ROWS)
    accs = []
    for r0 in range(0, rows, rb):
        ops = [o[r0:r0 + rb] for o in operands]

        def one(col0, acc):
            kc = sc_ref[r0:r0 + rb, pl.ds(col0, LANES)]
            return acc + jnp.where(pred(kc, col0, *ops), 1, 0).astype(I32)

        def grp(g, acc):
            base = pl.multiple_of(LANES + g * (cpg * LANES), LANES)
            for j in range(cpg):
                acc = one(base + j * LANES, acc)
            return acc

        accs.append(lax.fori_loop(0, ngrp, grp, one(0, jnp.zeros((rb, LANES), I32))))
    acc = accs[0] if len(accs) == 1 else jnp.concatenate(accs, axis=0)
    cnt = jnp.sum(acc.astype(F32), axis=1, keepdims=True)
    return jnp.broadcast_to(cnt, (rows, LANES))


def _select_threshold(sc_ref, ngrp, cpg, k):
    rows = sc_ref.shape[0]

    def take_bit(t, carry):
        prefix, n_ge = carry
        cand = prefix ^ lax.shift_left(jnp.int32(1), jnp.int32(31) - t)
        cnt = _count_rows(sc_ref, ngrp, cpg, lambda sc, c0, cf: sc >= cf, _key_to_float(cand))
        take = cnt >= k
        return jnp.where(take, cand, prefix), jnp.where(take, cnt, n_ge)

    key, n_ge = lax.fori_loop(0, 32, take_bit,
                              (jnp.full((rows, LANES), INT_MIN, I32), jnp.zeros((rows, LANES), F32)))
    return _drop_surplus(sc_ref, ngrp, cpg, k, key, n_ge)


def _select_threshold_unrolled(sc_ref, ngrp, cpg, k):
    rows = sc_ref.shape[0]
    scores = [sc_ref[:, c * LANES:(c + 1) * LANES] for c in range(1 + ngrp * cpg)]

    def count(cand):
        cf = _key_to_float(cand)
        acc = jnp.zeros((rows, LANES), I32)
        for sc in scores:
            acc = acc + jnp.where(sc >= cf, 1, 0)
        return jnp.broadcast_to(jnp.sum(acc.astype(F32), axis=1, keepdims=True), (rows, LANES))

    def two_bits(t, carry):
        prefix, n_ge = carry
        shift = jnp.int32(30) - 2 * t
        cands = [prefix ^ lax.shift_left(jnp.int32(d), shift) for d in (1, 2, 3)]
        for cand, cnt in zip(cands, [count(c) for c in cands]):
            take = cnt >= k
            prefix, n_ge = jnp.where(take, cand, prefix), jnp.where(take, cnt, n_ge)
        return prefix, n_ge

    key, n_ge = lax.fori_loop(0, 16, two_bits,
                              (jnp.full((rows, LANES), INT_MIN, I32), jnp.zeros((rows, LANES), F32)))
    return _drop_surplus(sc_ref, ngrp, cpg, k, key, n_ge)


def _drop_surplus(sc_ref, ngrp, cpg, k, key, n_ge):
    rows = sc_ref.shape[0]
    tau = _key_to_float(jnp.maximum(key, LOWEST_KEY))
    surplus = jnp.max(jnp.maximum(n_ge - k, 0.0))

    @pl.when(surplus > 0.0)
    def _():
        n_gt = _count_rows(sc_ref, ngrp, cpg, lambda sc, c0, tu: sc > tu, tau)
        allowed = k - n_gt
        lane = lax.broadcasted_iota(I32, (rows, LANES), 1)
        nbits = max(1, int(math.ceil(math.log2(sc_ref.shape[1] + 1))))

        def col_body(t, lim):
            cand = lim | lax.shift_left(jnp.int32(1), jnp.int32(nbits - 1) - t)
            cnt = _count_rows(
                sc_ref, ngrp, cpg,
                lambda sc, c0, tu, cd: (sc == tu) & (lax.broadcasted_iota(I32, sc.shape, 1) + c0 < cd), tau, cand)
            return jnp.where(cnt < allowed, cand, lim)

        lim = lax.fori_loop(0, nbits, col_body, jnp.zeros((rows, LANES), I32))

        def drop(c, carry):
            c0 = pl.multiple_of(c * LANES, LANES)
            sc = sc_ref[:, pl.ds(c0, LANES)]
            sc_ref[:, pl.ds(c0, LANES)] = jnp.where((sc == tau) & (lane + c0 > lim), -jnp.inf, sc)
            return carry

        lax.fori_loop(0, 1 + ngrp * cpg, drop, 0)

    return tau


SCORE_BLOCK = 256


def _masked_softmax_step(s, sel, v, c2, m, l, acc):
    s = jnp.where(sel, s, -jnp.inf)
    m_new = jnp.maximum(m, jnp.max(s, axis=1, keepdims=True))
    alpha = jnp.exp2((m - m_new) * c2)
    p = jnp.exp2((s - m_new) * c2)
    l = alpha * l + jnp.sum(p, axis=1, keepdims=True)
    acc = alpha * acc + jnp.dot(p.astype(BF16), v, preferred_element_type=F32)
    return m_new, l, acc


def _dsa_prompt_body(qb_ref, qi_ref, wi_ref, ki_ref, kb_ref, vb_ref, o_ref, sc_ref, *, tq, tk, topk):
    i = pl.program_id(1)
    n_full = (i * tq) // tk
    ngrp = n_full + 1
    cpg = tk // LANES
    lane = lax.broadcasted_iota(I32, (tq, LANES), 1)
    row = lax.broadcasted_iota(I32, (tq, LANES), 0)

    qi = qi_ref[0]
    wi = wi_ref[0]
    qst = jnp.concatenate([qi[:, h * D_I:(h + 1) * D_I] for h in range(H_I)], axis=0)
    wh = [jnp.broadcast_to(wi[:, h:h + 1], (tq, LANES)) for h in range(H_I)]

    def score(start, nk, col0, x0):
        kc = ki_ref[0, pl.ds(start, nk), :]
        r = lax.dot_general(qst, kc, NT_DIMS, preferred_element_type=F32)
        for j in range(nk // LANES):
            sc = jnp.zeros((tq, LANES), F32)
            for h in range(H_I):
                sc = sc + wh[h] * jnp.maximum(r[h * tq:(h + 1) * tq, j * LANES:(j + 1) * LANES], 0.0)
            sc = _rank_value(sc)
            if x0 is not False:
                valid = (lane < N_META) if x0 is None else (x0 + j * LANES + lane <= i * tq + row)
                sc = jnp.where(valid, sc, -jnp.inf)
            sc_ref[:, pl.ds(col0 + j * LANES, LANES)] = sc

    score(0, KEY_CHUNK, 0, None)

    def score_group(g, masked):
        sb = min(SCORE_BLOCK, tk)
        for jb in range(tk // sb):
            off = g * tk + jb * sb
            score(pl.multiple_of(N_META + off, 16), sb, pl.multiple_of(LANES + off, LANES),
                  off if masked else False)

    def score_loop(g, carry):
        score_group(g, False)
        return carry

    lax.fori_loop(0, n_full, score_loop, 0)
    score_group(n_full, True)

    tau = _select_threshold(sc_ref, ngrp, cpg, topk)

    q = qb_ref[0]
    qs = [q[:, h * DH_B:(h + 1) * DH_B] for h in range(H_B)]
    c2 = (DH_B ** -0.5) * math.log2(math.e)

    def att(start, nk, col0, carry):
        taub = tau if nk == LANES else jnp.concatenate([tau] * (nk // LANES), axis=1)
        sel = sc_ref[:, pl.ds(col0, nk)] >= taub
        kc = kb_ref[0, pl.ds(start, nk), :]
        vc = vb_ref[0, pl.ds(start, nk), :]
        hs = [slice(h * DH_B, (h + 1) * DH_B) for h in range(H_B)]
        scores = lambda h: lax.dot_general(qs[h], kc[:, hs[h]], NT_DIMS, preferred_element_type=F32)
        out, s_next = [], scores(0)
        for h in range(H_B):
            s, s_next = s_next, (scores(h + 1) if h + 1 < H_B else None)
            out.append(_masked_softmax_step(s, sel, vc[:, hs[h]], c2, *carry[h]))
        return tuple(out)

    init = tuple((jnp.full((tq, 1), NEG, F32), jnp.zeros((tq, 1), F32), jnp.zeros((tq, DH_B), F32))
                 for _ in range(H_B))
    carry = att(0, KEY_CHUNK, 0, init)
    carry = lax.fori_loop(
        0, ngrp, lambda g, cr: att(_group_start(g, tk), tk, pl.multiple_of(LANES + g * tk, LANES), cr), carry)
    for h in range(H_B):
        _, l, acc = carry[h]
        o_ref[0, :, h * DH_B:(h + 1) * DH_B] = (acc / l).astype(o_ref.dtype)


def _dsa_prompt(qb, qi, wi, ki, kb, vb, topk, tq, tk):
    B, S, _ = qb.shape
    T = kb.shape[1]
    nq = S // tq
    body = functools.partial(_dsa_prompt_body, tq=tq, tk=tk, topk=topk)
    blk = lambda b, i: (b, i, 0)
    res = lambda b, i: (b, 0, 0)
    once = pl.Buffered(1)
    return pl.pallas_call(
        body, grid=(B, nq),
        in_specs=[pl.BlockSpec((1, tq, W_B_DIM), blk), pl.BlockSpec((1, tq, H_I * D_I), blk),
                  pl.BlockSpec((1, tq, H_I), blk),
                  pl.BlockSpec((1, T, D_I), res, pipeline_mode=once),
                  pl.BlockSpec((1, T, W_B_DIM), res, pipeline_mode=once),
                  pl.BlockSpec((1, T, W_B_DIM), res, pipeline_mode=once)],
        out_specs=pl.BlockSpec((1, tq, W_B_DIM), blk),
        out_shape=jax.ShapeDtypeStruct((B, S, W_B_DIM), BF16),
        scratch_shapes=[pltpu.VMEM((tq, LANES + S), F32)],
        compiler_params=_cparams(("parallel", "arbitrary")), name="dsa_prompt",
    )(qb, qi, wi, ki, kb, vb)


def _div(x, d):
    assert d & (d - 1) == 0
    return x >> (d.bit_length() - 1)


def _mod(x, d):
    assert d & (d - 1) == 0
    return x & (d - 1)


def _pad_rows(a, rows):
    return jnp.concatenate([a, jnp.zeros((rows - a.shape[0], a.shape[1]), a.dtype)], axis=0)


def _page_specs(n_pages, rows, width):
    return [pl.BlockSpec((rows, width), functools.partial(lambda b, pt, p: (pt[b, p], 0), p=p))
            for p in range(n_pages)]


def _head_rows(q, heads):
    return jnp.concatenate([q[:, h * LANES:(h + 1) * LANES] for h in range(heads)], axis=0)


def _diff_sample_body(pt_ref, q_ref, kn_ref, vn_ref, *rest, n_pages, lam_init, ts):
    k_pages = rest[:n_pages]
    v_pages = rest[n_pages:2 * n_pages]
    lq1, lk1, lq2, lk2, g_ref, o_ref = rest[2 * n_pages:]
    lam = _diff_lambda(lq1, lk1, lq2, lk2, lam_init)
    rows = H_A * ts
    pw = KEY_CHUNK * H_A
    qh = _head_rows(q_ref[0].astype(F32), H_A)
    lane = lax.broadcasted_iota(I32, (rows, LANES), 1)
    qq = jnp.concatenate([jnp.where(lane < DH_A, qh, 0.0), jnp.where(lane >= DH_A, qh, 0.0)], axis=0).astype(BF16)
    kn = _pad_rows(kn_ref[0], LANES).astype(BF16)
    vn = _pad_rows(vn_ref[0], LANES).astype(BF16)
    r2 = lax.broadcasted_iota(I32, (2 * rows, pw), 0)
    c2 = lax.broadcasted_iota(I32, (2 * rows, pw), 1)
    same_head = _mod(_div(r2, ts), H_A) == _mod(c2, H_A)
    rn = lax.broadcasted_iota(I32, (2 * rows, LANES), 0)
    cn = lax.broadcasted_iota(I32, (2 * rows, LANES), 1)
    new_ok = (_mod(_div(rn, ts), H_A) == _mod(cn, H_A)) & (_div(cn, H_A) <= _mod(rn, ts)) & (cn < ts * H_A)
    s = jnp.concatenate(
        [jnp.where(same_head, lax.dot_general(qq, k_pages[p][...].astype(BF16), NT_DIMS,
                                              preferred_element_type=F32), -jnp.inf) for p in range(n_pages)]
        + [jnp.where(new_ok, lax.dot_general(qq, kn, NT_DIMS, preferred_element_type=F32), -jnp.inf)], axis=1)
    m = jnp.max(s, axis=1, keepdims=True)
    pr = jnp.exp(s - m)
    pr = pr / jnp.sum(pr, axis=1, keepdims=True)
    wgt = (pr[:rows] - lam * pr[rows:]).astype(BF16)
    o = jnp.dot(wgt[:, n_pages * pw:], vn, preferred_element_type=F32)
    for p in range(n_pages):
        o = o + jnp.dot(wgt[:, p * pw:(p + 1) * pw], v_pages[p][...].astype(BF16), preferred_element_type=F32)
    o = _diff_finish(o, g_ref[...], lam_init)
    for h in range(H_A):
        o_ref[0, :, h * LANES:(h + 1) * LANES] = o[h * ts:(h + 1) * ts].astype(o_ref.dtype)


def _diff_sample(qa, ka_new, va_new, pool_k, pool_v, page_table, lam_vecs, gain, lam_init):
    Bd, ts, _ = qa.shape
    n_pages = page_table.shape[1]
    pw = KEY_CHUNK * H_A
    body = functools.partial(_diff_sample_body, n_pages=n_pages, lam_init=lam_init, ts=ts)
    blk = lambda b, pt: (b, 0, 0)
    const2 = lambda b, pt: (0, 0)
    new_spec = pl.BlockSpec((1, ts * H_A, LANES), blk)
    grid_spec = pltpu.PrefetchScalarGridSpec(
        num_scalar_prefetch=1, grid=(Bd,),
        in_specs=[pl.BlockSpec((1, ts, W_A_DIM), blk), new_spec, new_spec]
                 + _page_specs(n_pages, pw, LANES) + _page_specs(n_pages, pw, LANES)
                 + [pl.BlockSpec(v.shape, const2) for v in lam_vecs] + [pl.BlockSpec(gain.shape, const2)],
        out_specs=pl.BlockSpec((1, ts, W_A_DIM), blk))
    return pl.pallas_call(
        body, grid_spec=grid_spec, out_shape=jax.ShapeDtypeStruct((Bd, ts, W_A_DIM), BF16),
        compiler_params=_cparams(("arbitrary",)), name="diff_sample",
    )(page_table, qa, ka_new, va_new, *([pool_k] * n_pages), *([pool_v] * n_pages), *lam_vecs, gain)


def _dsa_sample_body(pt_ref, qb_ref, qi_ref, wi_ref, kin_ref, kn_ref, vn_ref, *rest, n_pages, ts, topk, cpg):
    ki_pages = rest[:n_pages]
    k_pages = rest[n_pages:2 * n_pages]
    v_pages = rest[2 * n_pages:3 * n_pages]
    o_ref, sc_ref = rest[3 * n_pages:]
    rows_i = H_I * ts
    qi = qi_ref[0]
    wcol = jnp.broadcast_to(wi_ref[0], (rows_i, LANES))
    lane = lax.broadcasted_iota(I32, (ts, LANES), 1)
    row = lax.broadcasted_iota(I32, (ts, LANES), 0)

    def idx_scores(kc):
        r = lax.dot_general(qi, kc, NT_DIMS, preferred_element_type=F32)
        r = wcol * jnp.maximum(r, 0.0)
        sc = r[0:ts]
        for h in range(1, H_I):
            sc = sc + r[h * ts:(h + 1) * ts]
        return _rank_value(sc)

    for p in range(n_pages):
        sc_ref[:, p * LANES:(p + 1) * LANES] = idx_scores(ki_pages[p][0].astype(BF16))
    kin = _pad_rows(kin_ref[0], KEY_CHUNK).astype(BF16)
    sc_ref[:, n_pages * LANES:] = jnp.where((lane <= row) & (lane < ts), idx_scores(kin), -jnp.inf)

    tau = _select_threshold_unrolled(sc_ref, n_pages // cpg, cpg, topk)

    pw = KEY_CHUNK * H_B
    rows = H_B * ts
    expand = jnp.where(_div(lax.broadcasted_iota(I32, (KEY_CHUNK, pw), 1), H_B)
                       == lax.broadcasted_iota(I32, (KEY_CHUNK, pw), 0), 1.0, 0.0).astype(BF16)
    r2 = lax.broadcasted_iota(I32, (rows, pw), 0)
    c2i = lax.broadcasted_iota(I32, (rows, pw), 1)
    same_head = _div(r2, ts) == _mod(c2i, H_B)

    def selected(p):
        sel = jnp.where(sc_ref[:, p * LANES:(p + 1) * LANES] >= tau, 1.0, 0.0)
        sel = jnp.dot(_pad_rows(sel, 2 * ts).astype(BF16), expand, preferred_element_type=F32)[:ts]
        return same_head & (jnp.concatenate([sel] * H_B, axis=0) > 0.5)

    qh = _head_rows(qb_ref[0].astype(F32), H_B).astype(BF16)
    kn = _pad_rows(kn_ref[0], LANES).astype(BF16)
    vn = _pad_rows(vn_ref[0], LANES).astype(BF16)
    c2 = (DH_B ** -0.5) * math.log2(math.e)
    s = jnp.concatenate(
        [jnp.where(selected(p), lax.dot_general(qh, k_pages[p][...].astype(BF16), NT_DIMS,
                                                preferred_element_type=F32), -jnp.inf) for p in range(n_pages)]
        + [jnp.where(selected(n_pages)[:, :LANES], lax.dot_general(qh, kn, NT_DIMS, preferred_element_type=F32),
                     -jnp.inf)], axis=1)
    m = jnp.max(s, axis=1, keepdims=True)
    pr = jnp.exp2((s - m) * c2)
    pr = (pr / jnp.sum(pr, axis=1, keepdims=True)).astype(BF16)
    o = jnp.dot(pr[:, n_pages * pw:], vn, preferred_element_type=F32)
    for p in range(n_pages):
        o = o + jnp.dot(pr[:, p * pw:(p + 1) * pw], v_pages[p][...].astype(BF16), preferred_element_type=F32)
    for h in range(H_B):
        o_ref[0, :, h * LANES:(h + 1) * LANES] = o[h * ts:(h + 1) * ts].astype(o_ref.dtype)


def _dsa_sample(qb, qi_hs, wi_hs, ki_new, kb_new, vb_new, pool_ki, pool_k, pool_v, page_table, topk):
    Bd, ts, _ = qb.shape
    n_pages = page_table.shape[1]
    pw = KEY_CHUNK * H_B
    cpg = 4 if n_pages % 4 == 0 else 1
    body = functools.partial(_dsa_sample_body, n_pages=n_pages, ts=ts, topk=topk, cpg=cpg)
    blk = lambda b, pt: (b, 0, 0)
    new_spec = pl.BlockSpec((1, ts * H_B, LANES), blk)
    ki_specs = [pl.BlockSpec((1, KEY_CHUNK, D_I), functools.partial(lambda b, pt, p: (pt[b, p], 0, 0), p=p))
                for p in range(n_pages)]
    grid_spec = pltpu.PrefetchScalarGridSpec(
        num_scalar_prefetch=1, grid=(Bd,),
        in_specs=[pl.BlockSpec((1, ts, W_B_DIM), blk), pl.BlockSpec((1, H_I * ts, D_I), blk),
                  pl.BlockSpec((1, H_I * ts, 1), blk), pl.BlockSpec((1, ts, D_I), blk), new_spec, new_spec]
                 + ki_specs + _page_specs(n_pages, pw, LANES) + _page_specs(n_pages, pw, LANES),
        out_specs=pl.BlockSpec((1, ts, W_B_DIM), blk),
        scratch_shapes=[pltpu.VMEM((ts, (n_pages + 1) * LANES), F32)])
    return pl.pallas_call(
        body, grid_spec=grid_spec, out_shape=jax.ShapeDtypeStruct((Bd, ts, W_B_DIM), BF16),
        compiler_params=_cparams(("arbitrary",)), name="dsa_sample",
    )(page_table, qb, qi_hs, wi_hs, ki_new, kb_new, vb_new,
      *([pool_ki] * n_pages), *([pool_k] * n_pages), *([pool_v] * n_pages))


def _layer_norm(h, g, b):
    mu = jnp.mean(h, axis=1, keepdims=True)
    d = h - mu
    var = jnp.mean(d * d, axis=1, keepdims=True)
    return d * lax.rsqrt(var + LN_EPS) * g + b


def _sigmoid(x):
    return 1.0 / (1.0 + jnp.exp(-x))


def _route(logits):
    tm = logits.shape[0]
    lane_i = lax.broadcasted_iota(I32, (tm, LANES), 1)
    lane = lane_i.astype(F32)
    big = float(LANES)
    is_grp = (lane_i >= N_EXPERTS) & (lane_i < N_EXPERTS + N_GROUPS)
    lg = jnp.where(is_grp, logits, -jnp.inf)
    gmax = jnp.max(lg, axis=1, keepdims=True)
    g_val = 1.0 / jnp.sum(jnp.exp(lg - gmax), axis=1, keepdims=True)
    g_idx = jnp.min(jnp.where(lg == gmax, lane - N_EXPERTS, big), axis=1, keepdims=True)
    grp_of_lane = (lane_i >> int(math.log2(EXP_PER_GROUP))).astype(F32)
    in_grp = (lane_i < N_EXPERTS) & (grp_of_lane == g_idx)
    le = jnp.where(in_grp, logits, -jnp.inf)
    m1 = jnp.max(le, axis=1, keepdims=True)
    i1 = jnp.min(jnp.where(le == m1, lane, big), axis=1, keepdims=True)
    le2 = jnp.where(lane == i1, -jnp.inf, le)
    m2 = jnp.max(le2, axis=1, keepdims=True)
    i2 = jnp.min(jnp.where(le2 == m2, lane, big), axis=1, keepdims=True)
    e2 = jnp.exp(m2 - m1)
    gate1 = g_val / (1.0 + e2)
    gate2 = g_val * e2 / (1.0 + e2)
    return jnp.where(lane == i1, gate1, 0.0) + jnp.where(lane == i2, gate2, 0.0)


def _merge_body(x_ref, oa_ref, ob_ref, wa_ref, wb_ref, wg_ref, wo_ref, g1_ref, b1_ref, wr_ref, br_ref,
                x1_ref, comb_ref, *, alpha):
    x = x_ref[...]
    d = x.shape[1]
    ya = jnp.dot(oa_ref[...], wa_ref[...], preferred_element_type=F32)
    yb = jnp.dot(ob_ref[...], wb_ref[...], preferred_element_type=F32)
    gates = jnp.dot(x.astype(BF16), wg_ref[...], preferred_element_type=F32)
    m = _sigmoid(gates[:, :d]) * ya + _sigmoid(gates[:, d:]) * yb
    mix = jnp.dot(m.astype(BF16), wo_ref[...], preferred_element_type=F32)
    x1 = _layer_norm(alpha * x + mix, g1_ref[...], b1_ref[...])
    x1_ref[...] = x1
    logits = jnp.dot(x1, wr_ref[...], preferred_element_type=F32, precision=lax.Precision.HIGHEST) + br_ref[...]
    comb_ref[...] = _route(logits)


def _merge(x, oa, ob, wa, wb, wg, wo, g1, b1, wr, br, alpha, tm):
    n, d = x.shape
    body = functools.partial(_merge_body, alpha=alpha)
    rowblk = lambda w: pl.BlockSpec((tm, w), lambda i: (i, 0))
    consts = [wa, wb, wg, wo, g1, b1, wr, br]
    return pl.pallas_call(
        body, grid=(n // tm,),
        in_specs=[rowblk(d), rowblk(oa.shape[1]), rowblk(ob.shape[1])] + [_full_spec(c) for c in consts],
        out_specs=[rowblk(d), rowblk(LANES)],
        out_shape=[jax.ShapeDtypeStruct((n, d), F32), jax.ShapeDtypeStruct((n, LANES), F32)],
        compiler_params=_cparams(("parallel",)), name="merge_ln1_route",
    )(x, oa, ob, *consts)


def _moe_body(x_ref, comb_ref, wg_ref, wu_ref, wd_ref, g2_ref, b2_ref, y_ref, xb_ref, acc_ref, *, alpha):
    e = pl.program_id(1)

    @pl.when(e == 0)
    def _():
        xb_ref[...] = x_ref[...].astype(BF16)
        acc_ref[...] = jnp.zeros_like(acc_ref)

    xb = xb_ref[...]
    hg = jnp.dot(xb, wg_ref[0], preferred_element_type=F32)
    hu = jnp.dot(xb, wu_ref[0], preferred_element_type=F32)
    comb = comb_ref[...]
    lane = lax.broadcasted_iota(I32, comb.shape, 1)
    ce = jnp.sum(jnp.where(lane == e, comb, 0.0), axis=1, keepdims=True)
    act = hg * _sigmoid(hg) * hu * ce
    acc_ref[...] += jnp.dot(act.astype(BF16), wd_ref[0], preferred_element_type=F32)

    @pl.when(e == pl.num_programs(1) - 1)
    def _():
        y_ref[...] = _layer_norm(alpha * x_ref[...] + acc_ref[...], g2_ref[...], b2_ref[...])


def _moe(x1, comb, wg, wu, wd, g2, b2, alpha, tm):
    n, d = x1.shape
    ne, _, dff = wg.shape
    body = functools.partial(_moe_body, alpha=alpha)
    return pl.pallas_call(
        body, grid=(n // tm, ne),
        in_specs=[pl.BlockSpec((tm, d), lambda i, e: (i, 0)), pl.BlockSpec((tm, LANES), lambda i, e: (i, 0)),
                  pl.BlockSpec((1, d, dff), lambda i, e: (e, 0, 0)), pl.BlockSpec((1, d, dff), lambda i, e: (e, 0, 0)),
                  pl.BlockSpec((1, dff, d), lambda i, e: (e, 0, 0)),
                  pl.BlockSpec(g2.shape, lambda i, e: (0, 0)), pl.BlockSpec(b2.shape, lambda i, e: (0, 0))],
        out_specs=pl.BlockSpec((tm, d), lambda i, e: (i, 0)),
        out_shape=jax.ShapeDtypeStruct((n, d), F32),
        scratch_shapes=[pltpu.VMEM((tm, d), BF16), pltpu.VMEM((tm, d), F32)],
        compiler_params=_cparams(("parallel", "arbitrary")), name="moe_ln2",
    )(x1, comb, wg, wu, wd, g2, b2)


def _row_tile(n, target):
    t = min(n, target)
    while n % t:
        t //= 2
    return t


def kernel(x_prompt, x_sample, cache_diff_k, cache_diff_v, cache_dsa_k, cache_dsa_v, cache_idx_k, page_table, meta_tokens, w_in, lam_q1, lam_k1, lam_q2, lam_k2, diff_norm_g, w_branch_a, w_branch_b, w_out, ln1_g, ln1_b, w_grp, b_grp, w_exp, b_exp, w_gate, w_up, w_down, ln2_g, ln2_b):
    depth = w_in.shape[0]
    assert depth == 1, "single-layer step only"
    B, S, D = x_prompt.shape
    Bd, Ts, _ = x_sample.shape
    T = S + N_META
    n_pool, page = cache_diff_k.shape[1], cache_diff_k.shape[2]
    assert page == KEY_CHUNK and S % KEY_CHUNK == 0 and Ts % 8 == 0
    n_pages = page_table.shape[1]
    past = n_pages * page
    topk_p = min(TOPK_MAX, S // 4)
    topk_s = min(TOPK_MAX, (past + Ts) // 4)
    alpha = (2.0 * depth) ** 0.25
    lam_init = 0.8 - 0.6 * math.exp(-0.3 * 0)

    w = w_in[0]
    widths = list(IN_WIDTHS[:9]) + [D, D]
    cuts = np.concatenate([[0], np.cumsum(widths)])
    w_qa, w_ka, w_va, w_qb, w_kb, w_vb, w_qi, w_ki, w_wi, w_ga, w_gb = [
        w[:, int(cuts[j]):int(cuts[j + 1])].astype(BF16) for j in range(11)]
    w_ki = jnp.pad(w_ki, ((0, 0), (0, LANES - D_I)))
    w_wi = jnp.pad(w_wi, ((0, 0), (0, LANES - H_I)))
    w_g = jnp.concatenate([w_ga, w_gb], axis=1)
    lam_vecs = [v.astype(F32) for v in (lam_q1, lam_k1, lam_q2, lam_k2)]
    gain = diff_norm_g.astype(F32)
    w_r = jnp.pad(jnp.concatenate([w_exp[0], w_grp[0]], axis=1), ((0, 0), (0, LANES - N_EXPERTS - N_GROUPS)))
    b_r = jnp.pad(jnp.concatenate([b_exp[0], b_grp[0]])[None, :], ((0, 0), (0, LANES - N_EXPERTS - N_GROUPS)))
    wa16, wb16, wo16 = w_branch_a[0].astype(BF16), w_branch_b[0].astype(BF16), w_out[0].astype(BF16)
    wg16, wu16, wd16 = w_gate[0].astype(BF16), w_up[0].astype(BF16), w_down[0].astype(BF16)

    k_segs = [(w_ka, "A", [(F32, W_A_DIM, 1.0, H_A), (BF16, W_A_DIM, 1.0, 0)]),
              (w_va, None, [(F32, W_A_DIM, 1.0, H_A), (BF16, W_A_DIM, 1.0, 0)]),
              (w_kb, "B", [(F32, W_B_DIM, 1.0, H_B), (BF16, W_B_DIM, 1.0, 0)]),
              (w_vb, None, [(F32, W_B_DIM, 1.0, H_B), (BF16, W_B_DIM, 1.0, 0)]),
              (w_ki, "A", [(F32, D_I, 1.0, 0), (BF16, D_I, 1.0, 0)])]
    q_segs = [(w_qa, "A", [(BF16, W_A_DIM, DH_A ** -0.5, 0)]),
              (w_qb, "B", [(BF16, W_B_DIM, 1.0, 0)]),
              (w_qi, "A", [(BF16, H_I * D_I, D_I ** -0.5, 0)]),
              (w_wi, None, [(F32, H_I, H_I ** -0.5, 0)])]

    pos_p = jnp.arange(T, dtype=jnp.int32)
    tabs_p = _rope_tables(pos_p, DH_A // 4, DH_A) + _rope_tables(pos_p, DH_B // 4, DH_B)
    tabs_q = tuple(t[N_META:] for t in tabs_p)
    (ka_p, ka16, va_p, va16, kb_p, kb16, vb_p, vb16, ki_p, ki16) = _project(
        x_prompt, tabs_p, k_segs, _row_tile(S, 512), "proj_k_prompt", prefix=meta_tokens.astype(x_prompt.dtype))
    qa16, qb16, qi16, wi_p = _project(x_prompt, tabs_q, q_segs, _row_tile(S, 512), "proj_q_prompt")
    tk = next(t for t in (512, 256, KEY_CHUNK) if S % t == 0)
    tq = min(256, tk)
    oa_p = _diff_prompt(qa16, ka16, va16, lam_vecs, gain, lam_init, tq, tk)
    ob_p = _dsa_prompt(qb16, qi16, wi_p, ki16, kb16, vb16, topk_p, tq, tk)

    ns = Bd * Ts
    pos_s = past + (jnp.arange(ns, dtype=jnp.int32) % Ts)
    tabs_s = _rope_tables(pos_s, DH_A // 4, DH_A) + _rope_tables(pos_s, DH_B // 4, DH_B)
    s_segs = [(wt, rope, [o for o in outs if o[0] == F32]) for (wt, rope, outs) in k_segs] + q_segs
    (ka_s, va_s, kb_s, vb_s, ki_s, qa_s, qb_s, qi_s, wi_s) = _project(
        x_sample.reshape(1, ns, D), tabs_s, s_segs, _row_tile(ns, 512), "proj_sample")
    seq = lambda a: a.reshape(Bd, -1, a.shape[-1])
    pool = lambda c: c.reshape(-1, c.shape[-1])
    oa_s = _diff_sample(seq(qa_s), seq(ka_s), seq(va_s), pool(cache_diff_k), pool(cache_diff_v),
                        page_table, lam_vecs, gain, lam_init)
    qi_hs = seq(qi_s).reshape(Bd, Ts, H_I, D_I).transpose(0, 2, 1, 3).reshape(Bd, H_I * Ts, D_I)
    wi_hs = seq(wi_s).transpose(0, 2, 1).reshape(Bd, H_I * Ts, 1)
    ob_s = _dsa_sample(seq(qb_s), qi_hs, wi_hs, seq(ki_s), seq(kb_s), seq(vb_s),
                       cache_idx_k.reshape(n_pool, page, D_I), pool(cache_dsa_k), pool(cache_dsa_v), page_table, topk_s)

    def tail(x2d, oa2d, ob2d):
        n = x2d.shape[0]
        x1, comb = _merge(x2d, oa2d, ob2d, wa16, wb16, w_g, wo16, ln1_g, ln1_b, w_r, b_r, alpha, _row_tile(n, 512))
        return _moe(x1, comb, wg16, wu16, wd16, ln2_g, ln2_b, alpha, _row_tile(n, 1024))

    y_prompt = tail(x_prompt.reshape(B * S, D), oa_p.reshape(B * S, -1), ob_p.reshape(B * S, -1)).reshape(B, S, D)
    y_sample = tail(x_sample.reshape(ns, D), oa_s.reshape(ns, -1), ob_s.reshape(ns, -1)).reshape(Bd, Ts, D)

    cache_p = lambda a, h: a.reshape(1, B, T, h, LANES)
    cache_s = lambda a, h: a.reshape(1, Bd, Ts, h, LANES)
    return (y_prompt, y_sample,
            cache_p(ka_p, H_A), cache_p(va_p, H_A), cache_p(kb_p, H_B), cache_p(vb_p, H_B), ki_p[None],
            cache_s(ka_s, H_A), cache_s(va_s, H_A), cache_s(kb_s, H_B), cache_s(vb_s, H_B), seq(ki_s)[None])
```
